```python
import math
import jax, jax.numpy as jnp
from jax import lax
import numpy as np

D_MODEL = 2048
BATCH = 1
SEQ = 8192
DEPTH = 2
DEC_BATCH = 32
DEC_SEQ = 1
PAST_LEN = 8192
PAGE_SIZE = 128

N_A = DEPTH // 2
N_B = DEPTH - N_A
N_DENSE = (DEPTH + 1) // 2
N_MOE = DEPTH // 2
HEAD_DIM = 64
N_HEADS = D_MODEL // (2 * HEAD_DIM)
V_DIM = 2 * HEAD_DIM
CONV_W = 31
D_FF = 5632
N_EXPERTS = 8
TOP_K = 2
D_EXPERT = D_FF // 4
PLE_DIM = 256
ROPE_THETA = 10000.0
LN_EPS = 1e-5
Q_BLOCK = 128
DN_ALPHA = (2.0 * DEPTH) ** 0.25
DN_BETA = (8.0 * DEPTH) ** -0.25
ATTN_SCALE = HEAD_DIM ** -0.5

kernel_name = 'yoco_conformer_diffattn_moe_step'


def layer_norm(x, g, b):
    xf = x.astype(jnp.float32)
    mu = jnp.mean(xf, axis=-1, keepdims=True)
    var = jnp.mean(jnp.square(xf - mu), axis=-1, keepdims=True)
    y = (xf - mu) * lax.rsqrt(var + LN_EPS) * g.astype(jnp.float32) + b.astype(jnp.float32)
    return y.astype(x.dtype)


def rms_norm(x, g):
    xf = x.astype(jnp.float32)
    y = xf * lax.rsqrt(jnp.mean(jnp.square(xf), axis=-1, keepdims=True) + LN_EPS)
    return y * g.astype(jnp.float32)


def rope(x, pos):
    half = HEAD_DIM // 2
    inv = ROPE_THETA ** (-jnp.arange(half, dtype=jnp.float32) / half)
    ang = pos.astype(jnp.float32)[:, None] * inv[None, :]
    shape = (1, ang.shape[0]) + (1,) * (x.ndim - 3) + (half,)
    cos = jnp.cos(ang).reshape(shape)
    sin = jnp.sin(ang).reshape(shape)
    xf = x.astype(jnp.float32)
    x1, x2 = xf[..., :half], xf[..., half:]
    return jnp.concatenate([x1 * cos - x2 * sin, x2 * cos + x1 * sin], axis=-1).astype(x.dtype)


def conformer_conv(x, hist, w_pw1, b_pw1, w_dw, b_dw, ln_g, ln_b, w_pw2):
    h = x @ w_pw1 + b_pw1
    u = h[..., :D_MODEL] * jax.nn.sigmoid(h[..., D_MODEL:])
    full = jnp.concatenate([hist.astype(u.dtype), u], axis=1)
    y = lax.conv_general_dilated(full, w_dw[:, None, :], window_strides=(1,), padding='VALID',
                                 dimension_numbers=('NWC', 'WIO', 'NWC'),
                                 feature_group_count=D_MODEL) + b_dw
    y = jax.nn.silu(layer_norm(y, ln_g, ln_b))
    return y @ w_pw2, full[:, -(CONV_W - 1):]


def swiglu(x, w_gate, w_up, w_down):
    return (jax.nn.silu(x @ w_gate) * (x @ w_up)) @ w_down


def moe_swiglu(x, w_router, w_gate, w_up, w_down):
    logits = (x @ w_router).astype(jnp.float32)
    top_v, top_i = lax.top_k(logits, TOP_K)
    gates = jax.nn.softmax(top_v, axis=-1)
    comb = jnp.einsum('bske,bsk->bse', jax.nn.one_hot(top_i, N_EXPERTS, dtype=jnp.float32), gates)
    h = jax.nn.silu(jnp.einsum('bsd,edf->bsef', x, w_gate)) * jnp.einsum('bsd,edf->bsef', x, w_up)
    h = h * comb[..., None].astype(h.dtype)
    return jnp.einsum('bsef,efd->bsd', h, w_down)


def per_layer_embed(x, p, w_proj, w_gate):
    return x + jax.nn.sigmoid(x @ w_gate) * (p @ w_proj)


def shared_kv(x, pos, w_k, w_v):
    B, S = x.shape[:2]
    k = rope((x @ w_k).reshape(B, S, N_HEADS, 2, HEAD_DIM), pos)
    v = (x @ w_v).reshape(B, S, N_HEADS, V_DIM)
    return k, v


def diff_lambda(lam_p, lam_init):
    lp = lam_p.astype(jnp.float32)
    return jnp.exp(jnp.sum(lp[0] * lp[1])) - jnp.exp(jnp.sum(lp[2] * lp[3])) + lam_init


def diff_block_stats(q, k, v, mask):
    s = jnp.einsum('bqhmd,bkhmd->bhmqk', q, k).astype(jnp.float32) * ATTN_SCALE
    if mask is not None:
        s = jnp.where(mask, s, -jnp.inf)
    m = jnp.max(s, axis=-1)
    p = jnp.exp(s - m[..., None])
    l = jnp.sum(p, axis=-1)
    acc = jnp.einsum('bhmqk,bkhe->bhmqe', p.astype(v.dtype), v).astype(jnp.float32)
    return m, l, acc


def merge_stats(a, b):
    m1, l1, acc1 = a
    m2, l2, acc2 = b
    m = jnp.maximum(m1, m2)
    c1, c2 = jnp.exp(m1 - m), jnp.exp(m2 - m)
    return m, l1 * c1 + l2 * c2, acc1 * c1[..., None] + acc2 * c2[..., None]


def diff_finalize(l, acc, lam, sub_g, lam_init, dtype):
    o = acc / l[..., None]
    o = o[:, :, 0] - lam * o[:, :, 1]
    o = rms_norm(o, sub_g) * (1.0 - lam_init)
    B, H, Sq, E = o.shape
    return o.transpose(0, 2, 1, 3).reshape(B, Sq, H * E).astype(dtype)


def prompt_diff_attn(q, k, v, lam, sub_g, lam_init):
    B, S = q.shape[:2]
    nb = S // Q_BLOCK
    qb = jnp.moveaxis(q.reshape((B, nb, Q_BLOCK) + q.shape[2:]), 1, 0)
    kpos = jnp.arange(S)

    def one_block(args):
        qi, bi = args
        qpos = bi * Q_BLOCK + jnp.arange(Q_BLOCK)
        mask = kpos[None, :] <= qpos[:, None]
        _, l, acc = diff_block_stats(qi, k, v, mask)
        return diff_finalize(l, acc, lam, sub_g, lam_init, q.dtype)

    out = lax.map(one_block, (qb, jnp.arange(nb)))
    return jnp.moveaxis(out, 0, 1).reshape(B, S, -1)


def sample_diff_attn(q, k_new, v_new, cache_k, cache_v, page_table, lam, sub_g, lam_init):
    Bd, Sd = q.shape[:2]
    self_mask = jnp.tril(jnp.ones((Sd, Sd), dtype=bool))
    init = diff_block_stats(q, k_new, v_new, self_mask)

    def page_step(carry, phys):
        kp = cache_k[phys].reshape(Bd, -1, N_HEADS, 2, HEAD_DIM)
        vp = cache_v[phys]
        return merge_stats(carry, diff_block_stats(q, kp, vp, None)), None

    (m, l, acc), _ = lax.scan(page_step, init, page_table.T)
    return diff_finalize(l, acc, lam, sub_g, lam_init, q.dtype)


def setup_inputs(seed: int = 0) -> dict:
    key = jax.random.key(seed)
    ks = jax.random.split(key, 32)
    f32 = jnp.float32
    D = D_MODEL

    def nrm(i, shape, scale=1.0):
        return jax.random.normal(ks[i], shape, f32) * scale

    n_pages = PAST_LEN // PAGE_SIZE
    used = DEC_BATCH * n_pages
    n_phys = used + max(1, used // 4)
    page_table = jax.random.permutation(ks[5], n_phys)[:used].reshape(DEC_BATCH, n_pages).astype(jnp.int32)
    return {
        'x_prompt': nrm(0, (BATCH, SEQ, D)),
        'x_sample': nrm(1, (DEC_BATCH, DEC_SEQ, D)),
        'state_conv': nrm(2, (N_A, DEC_BATCH, CONV_W - 1, D), 0.5),
        'cache_k': nrm(3, (n_phys, PAGE_SIZE, 2 * N_HEADS, HEAD_DIM)),
        'cache_v': nrm(4, (n_phys, PAGE_SIZE, N_HEADS, V_DIM)),
        'page_table': page_table,
        'p_prompt': nrm(6, (DEPTH, BATCH, SEQ, PLE_DIM)),
        'p_sample': nrm(7, (DEPTH, DEC_BATCH, DEC_SEQ, PLE_DIM)),
        'conv_w_pw1': nrm(8, (N_A, D, 2 * D), D ** -0.5),
        'conv_b_pw1': nrm(9, (N_A, 2 * D), 0.01),
        'conv_w_dw': nrm(10, (N_A, CONV_W, D), CONV_W ** -0.5),
        'conv_b_dw': nrm(11, (N_A, D), 0.01),
        'conv_ln_g': 1.0 + nrm(12, (N_A, D), 0.01),
        'conv_ln_b': nrm(13, (N_A, D), 0.01),
        'conv_w_pw2': nrm(14, (N_A, D, D), D ** -0.5 * DN_BETA),
        'kv_w_k': nrm(15, (D, 2 * N_HEADS * HEAD_DIM), D ** -0.5),
        'kv_w_v': nrm(16, (D, N_HEADS * V_DIM), D ** -0.5 * DN_BETA),
        'attn_w_q': nrm(17, (N_B, D, 2 * N_HEADS * HEAD_DIM), D ** -0.5),
        'attn_lambda': nrm(18, (N_B, 4, HEAD_DIM), 0.1),
        'attn_sub_g': 1.0 + nrm(19, (N_B, V_DIM), 0.01),
        'attn_w_o': nrm(20, (N_B, N_HEADS * V_DIM, D), (N_HEADS * V_DIM) ** -0.5 * DN_BETA),
        'ffn_w_gate': nrm(21, (N_DENSE, D, D_FF), D ** -0.5),
        'ffn_w_up': nrm(22, (N_DENSE, D, D_FF), D ** -0.5),
        'ffn_w_down': nrm(23, (N_DENSE, D_FF, D), D_FF ** -0.5 * DN_BETA),
        'moe_w_router': nrm(24, (N_MOE, D, N_EXPERTS), D ** -0.5),
        'moe_w_gate': nrm(25, (N_MOE, N_EXPERTS, D, D_EXPERT), D ** -0.5),
        'moe_w_up': nrm(26, (N_MOE, N_EXPERTS, D, D_EXPERT), D ** -0.5),
        'moe_w_down': nrm(27, (N_MOE, N_EXPERTS, D_EXPERT, D), D_EXPERT ** -0.5 * DN_BETA),
        'post_ln_g': 1.0 + nrm(28, (DEPTH, 2, D), 0.01),
        'post_ln_b': nrm(29, (DEPTH, 2, D), 0.01),
        'ple_w_proj': nrm(30, (DEPTH, PLE_DIM, D), PLE_DIM ** -0.5),
        'ple_w_gate': nrm(31, (DEPTH, D, D), D ** -0.5),
    }


def reference(x_prompt, x_sample, state_conv, cache_k, cache_v, page_table, p_prompt, p_sample,
              conv_w_pw1, conv_b_pw1, conv_w_dw, conv_b_dw, conv_ln_g, conv_ln_b, conv_w_pw2,
              kv_w_k, kv_w_v, attn_w_q, attn_lambda, attn_sub_g, attn_w_o,
              ffn_w_gate, ffn_w_up, ffn_w_down, moe_w_router, moe_w_gate, moe_w_up, moe_w_down,
              post_ln_g, post_ln_b, ple_w_proj, ple_w_gate):
    Bp, Sp = x_prompt.shape[:2]
    Bd, Sd = x_sample.shape[:2]
    past = page_table.shape[1] * cache_k.shape[1]
    pos_p = jnp.arange(Sp)
    pos_d = past + jnp.arange(Sd)
    xp, xd = x_prompt, x_sample
    conv_p, conv_d = [], []
    for i in range(DEPTH):
        if i < N_A:
            cw = (conv_w_pw1[i], conv_b_pw1[i], conv_w_dw[i], conv_b_dw[i], conv_ln_g[i], conv_ln_b[i], conv_w_pw2[i])
            mp, hp = conformer_conv(xp, jnp.zeros((Bp, CONV_W - 1, D_MODEL), xp.dtype), *cw)
            md, hd = conformer_conv(xd, state_conv[i], *cw)
            conv_p.append(hp)
            conv_d.append(hd)
        else:
            j = i - N_A
            lam_init = 0.8 - 0.6 * math.exp(-0.3 * i)
            lam = diff_lambda(attn_lambda[j], lam_init)
            qp = rope((xp @ attn_w_q[j]).reshape(Bp, Sp, N_HEADS, 2, HEAD_DIM), pos_p)
            qd = rope((xd @ attn_w_q[j]).reshape(Bd, Sd, N_HEADS, 2, HEAD_DIM), pos_d)
            mp = prompt_diff_attn(qp, k_p, v_p, lam, attn_sub_g[j], lam_init) @ attn_w_o[j]
            md = sample_diff_attn(qd, k_d, v_d, cache_k, cache_v, page_table, lam, attn_sub_g[j], lam_init) @ attn_w_o[j]
        xp = layer_norm(DN_ALPHA * xp + mp, post_ln_g[i, 0], post_ln_b[i, 0])
        xd = layer_norm(DN_ALPHA * xd + md, post_ln_g[i, 0], post_ln_b[i, 0])
        if i % 2 == 0:
            fw = (ffn_w_gate[i // 2], ffn_w_up[i // 2], ffn_w_down[i // 2])
            fp, fd = swiglu(xp, *fw), swiglu(xd, *fw)
        else:
            mw = (moe_w_router[i // 2], moe_w_gate[i // 2], moe_w_up[i // 2], moe_w_down[i // 2])
            fp, fd = moe_swiglu(xp, *mw), moe_swiglu(xd, *mw)
        xp = layer_norm(DN_ALPHA * xp + fp, post_ln_g[i, 1], post_ln_b[i, 1])
        xd = layer_norm(DN_ALPHA * xd + fd, post_ln_g[i, 1], post_ln_b[i, 1])
        xp = per_layer_embed(xp, p_prompt[i], ple_w_proj[i], ple_w_gate[i])
        xd = per_layer_embed(xd, p_sample[i], ple_w_proj[i], ple_w_gate[i])
        if i == N_A - 1:
            k_p, v_p = shared_kv(xp, pos_p, kv_w_k, kv_w_v)
            k_d, v_d = shared_kv(xd, pos_d, kv_w_k, kv_w_v)
    y_prompt, y_sample = xp, xd
    conv_prompt = jnp.stack(conv_p)
    conv_sample = jnp.stack(conv_d)
    k_prompt = k_p.reshape(Bp, Sp, 2 * N_HEADS, HEAD_DIM)
    k_sample = k_d.reshape(Bd, Sd, 2 * N_HEADS, HEAD_DIM)
    return (y_prompt, y_sample, conv_prompt, conv_sample, k_prompt, v_p, k_sample, v_d)
```

```python
import functools
import math

import jax
import jax.numpy as jnp
from jax import lax
from jax.experimental import pallas as pl
from jax.experimental.pallas import tpu as pltpu

HEAD_DIM = 64
V_DIM = 2 * HEAD_DIM
CONV_W = 31
TOP_K = 2
ROPE_THETA = 10000.0
LN_EPS = 1e-5
ATTN_SCALE = HEAD_DIM ** -0.5

LANES = 128
SUBLANES = 8
HALO = 32
VMEM_LIMIT = 56 * 1024 * 1024

F32 = jnp.float32
BF16 = jnp.bfloat16


def _cparams(*sem):
    return pltpu.CompilerParams(dimension_semantics=sem, vmem_limit_bytes=VMEM_LIMIT)


def _dot(a, b):
    return jnp.dot(a, b, preferred_element_type=F32)


def _ln(x, g, b):
    mu = jnp.mean(x, axis=-1, keepdims=True)
    xc = x - mu
    var = jnp.mean(xc * xc, axis=-1, keepdims=True)
    return xc * lax.rsqrt(var + LN_EPS) * g + b


def _sigmoid(x):
    return 1.0 / (1.0 + jnp.exp(-x))


def _silu(x):
    return x * _sigmoid(x)


def _glu_kernel(x_ref, wa_ref, wg_ref, ba_ref, bg_ref, u_ref, xb_ref):
    @pl.when(pl.program_id(1) == 0)
    def _():
        xb_ref[...] = x_ref[...].astype(BF16)

    xb = xb_ref[...]
    a = _dot(xb, wa_ref[...]) + ba_ref[...]
    g = _dot(xb, wg_ref[...]) + bg_ref[...]
    u_ref[...] = a * _sigmoid(g)


def _glu(x, w1, b1, tm, tn):
    m, d = x.shape
    nj = d // tn
    return pl.pallas_call(
        _glu_kernel,
        grid=(m // tm, nj),
        in_specs=[
            pl.BlockSpec((tm, d), lambda i, j: (i, 0)),
            pl.BlockSpec((d, tn), lambda i, j: (0, j)),
            pl.BlockSpec((d, tn), lambda i, j: (0, j + nj)),
            pl.BlockSpec((1, tn), lambda i, j: (0, j)),
            pl.BlockSpec((1, tn), lambda i, j: (0, j + nj)),
        ],
        out_specs=pl.BlockSpec((tm, tn), lambda i, j: (i, j)),
        out_shape=jax.ShapeDtypeStruct((m, d), F32),
        scratch_shapes=[pltpu.VMEM((tm, d), BF16)],
        compiler_params=_cparams("parallel", "arbitrary"),
        name="glu",
    )(x, w1, w1, b1, b1)


def _mix_tail(y, x, g_ref, b_ref, w2_ref, pg_ref, pb_ref, alpha):
    y = _silu(_ln(y, g_ref[...], b_ref[...]))
    m = _dot(y.astype(BF16), w2_ref[...])
    return _ln(alpha * x + m, pg_ref[...], pb_ref[...])


def _conv_prompt_kernel(uprev_ref, u_ref, x_ref, wdw_ref, bdw_ref, g_ref, b_ref, w2_ref,
                        pg_ref, pb_ref, o_ref, full_ref, y_ref, *, alpha, rows):
    tm, d = u_ref.shape
    first = pl.program_id(0) == 0
    full_ref[0:HALO, :] = jnp.where(first, 0.0, uprev_ref[...])
    full_ref[HALO:, :] = u_ref[...]
    off = HALO - (CONV_W - 1)

    def col_body(c, carry):
        cs = pl.ds(pl.multiple_of(c * LANES, LANES), LANES)
        w = wdw_ref[:, cs]
        bias = bdw_ref[:, cs]

        def row_body(r, carry2):
            base = pl.multiple_of(r * rows, rows)
            win = full_ref[pl.ds(base, rows + HALO), cs]
            acc = jnp.zeros((rows, LANES), F32)
            for res in range(SUBLANES):
                sh = win if res == 0 else pltpu.roll(win, rows + HALO - res, 0)
                for t in range(CONV_W):
                    if (off + t) % SUBLANES == res:
                        a8 = off + t - res
                        acc = acc + sh[a8:a8 + rows, :] * w[t:t + 1, :]
            y_ref[pl.ds(base, rows), cs] = acc + bias
            return carry2

        return lax.fori_loop(0, tm // rows, row_body, carry)

    lax.fori_loop(0, d // LANES, col_body, 0)
    o_ref[...] = _mix_tail(y_ref[...], x_ref[...], g_ref, b_ref, w2_ref, pg_ref, pb_ref, alpha)


def _conv_prompt(u, x, wdw, bdw, g, b, w2, pg, pb, alpha, tm):
    m, d = x.shape
    nh = tm // HALO
    vec = pl.BlockSpec((1, d), lambda i: (0, 0))
    return pl.pallas_call(
        functools.partial(_conv_prompt_kernel, alpha=alpha, rows=32),
        grid=(m // tm,),
        in_specs=[
            pl.BlockSpec((HALO, d), lambda i: (jnp.maximum(i * nh - 1, 0), 0)),
            pl.BlockSpec((tm, d), lambda i: (i, 0)),
            pl.BlockSpec((tm, d), lambda i: (i, 0)),
            pl.BlockSpec((CONV_W, d), lambda i: (0, 0)),
            vec, vec, vec,
            pl.BlockSpec((d, d), lambda i: (0, 0)),
            vec, vec,
        ],
        out_specs=pl.BlockSpec((tm, d), lambda i: (i, 0)),
        out_shape=jax.ShapeDtypeStruct((m, d), F32),
        scratch_shapes=[pltpu.VMEM((tm + HALO, d), F32), pltpu.VMEM((tm, d), F32)],
        compiler_params=_cparams("parallel"),
        name="conv_prompt",
    )(u, u, x, wdw, bdw, g, b, w2, pg, pb)


def _conv_sample_kernel(hist_ref, u_ref, x_ref, wdw_ref, bdw_ref, g_ref, b_ref, w2_ref,
                        pg_ref, pb_ref, o_ref, *, alpha):
    y = u_ref[...] * wdw_ref[CONV_W - 1:CONV_W, :]
    for t in range(CONV_W - 1):
        y = y + hist_ref[t] * wdw_ref[t:t + 1, :]
    y = y + bdw_ref[...]
    o_ref[...] = _mix_tail(y, x_ref[...], g_ref, b_ref, w2_ref, pg_ref, pb_ref, alpha)


def _conv_sample(hist, u, x, wdw, bdw, g, b, w2, pg, pb, alpha):
    m, d = x.shape
    return pl.pallas_call(
        functools.partial(_conv_sample_kernel, alpha=alpha),
        out_shape=jax.ShapeDtypeStruct((m, d), F32),
        compiler_params=pltpu.CompilerParams(vmem_limit_bytes=VMEM_LIMIT),
        name="conv_sample",
    )(hist, u, x, wdw, bdw, g, b, w2, pg, pb)


def _swiglu_kernel(x_ref, wg_ref, wu_ref, wd_ref, pg_ref, pb_ref, o_ref, xb_ref, *, alpha):
    f = pl.program_id(1)

    @pl.when(f == 0)
    def _():
        xb_ref[...] = x_ref[...].astype(BF16)

    xb = xb_ref[...]
    h = _silu(_dot(xb, wg_ref[...])) * _dot(xb, wu_ref[...])
    part = _dot(h.astype(BF16), wd_ref[...])

    @pl.when(f == 0)
    def _():
        o_ref[...] = part

    @pl.when(f > 0)
    def _():
        o_ref[...] += part

    @pl.when(f == pl.num_programs(1) - 1)
    def _():
        o_ref[...] = _ln(alpha * x_ref[...] + o_ref[...], pg_ref[...], pb_ref[...])


def _swiglu(x, wg, wu, wd, pg, pb, alpha, tm, tf):
    m, d = x.shape
    dff = wg.shape[1]
    vec = pl.BlockSpec((1, d), lambda i, f: (0, 0))
    return pl.pallas_call(
        functools.partial(_swiglu_kernel, alpha=alpha),
        grid=(m // tm, dff // tf),
        in_specs=[
            pl.BlockSpec((tm, d), lambda i, f: (i, 0)),
            pl.BlockSpec((d, tf), lambda i, f: (0, f)),
            pl.BlockSpec((d, tf), lambda i, f: (0, f)),
            pl.BlockSpec((tf, d), lambda i, f: (f, 0)),
            vec, vec,
        ],
        out_specs=pl.BlockSpec((tm, d), lambda i, f: (i, 0)),
        out_shape=jax.ShapeDtypeStruct((m, d), F32),
        scratch_shapes=[pltpu.VMEM((tm, d), BF16)],
        compiler_params=_cparams("parallel", "arbitrary"),
        name="swiglu",
    )(x, wg, wu, wd, pg, pb)


def _ple_kernel(x_ref, xc_ref, p_ref, wg_ref, wp_ref, o_ref, xb_ref):
    @pl.when(pl.program_id(1) == 0)
    def _():
        xb_ref[...] = x_ref[...].astype(BF16)

    gate = _sigmoid(_dot(xb_ref[...], wg_ref[...]))
    o_ref[...] = xc_ref[...] + gate * _dot(p_ref[...].astype(BF16), wp_ref[...])


def _ple(x, p, wg, wp, tm, tn):
    m, d = x.shape
    pd = p.shape[1]
    return pl.pallas_call(
        _ple_kernel,
        grid=(m // tm, d // tn),
        in_specs=[
            pl.BlockSpec((tm, d), lambda i, j: (i, 0)),
            pl.BlockSpec((tm, tn), lambda i, j: (i, j)),
            pl.BlockSpec((tm, pd), lambda i, j: (i, 0)),
            pl.BlockSpec((d, tn), lambda i, j: (0, j)),
            pl.BlockSpec((pd, tn), lambda i, j: (0, j)),
        ],
        out_specs=pl.BlockSpec((tm, tn), lambda i, j: (i, j)),
        out_shape=jax.ShapeDtypeStruct((m, d), F32),
        scratch_shapes=[pltpu.VMEM((tm, d), BF16)],
        compiler_params=_cparams("parallel", "arbitrary"),
        name="ple",
    )(x, x, p, wg, wp)


def _rope_tables(pos):
    half = HEAD_DIM // 2
    inv = ROPE_THETA ** (-jnp.arange(half, dtype=F32) / half)
    ang = pos.astype(F32)[:, None] * inv[None, :]
    cos, sin = jnp.cos(ang), jnp.sin(ang)
    cos_t = jnp.concatenate([cos, cos, cos, cos], axis=-1)
    sin_t = jnp.concatenate([-sin, sin, -sin, sin], axis=-1)
    return cos_t, sin_t


def _rope_tile(h, cos, sin, first_half):
    partner = jnp.where(first_half, pltpu.roll(h, LANES - HEAD_DIM // 2, 1),
                        pltpu.roll(h, HEAD_DIM // 2, 1))
    return h * cos + partner * sin


def _proj_kernel(x_ref, wk_ref, wv_ref, wq_ref, cos_ref, sin_ref,
                 k_ref, v_ref, kb_ref, vb_ref, qb_ref, xb_ref):
    @pl.when(pl.program_id(1) == 0)
    def _():
        xb_ref[...] = x_ref[...].astype(BF16)

    xb = xb_ref[...]
    tm, tn = k_ref.shape
    cos, sin = cos_ref[...], sin_ref[...]
    lane = lax.broadcasted_iota(jnp.int32, (tm, LANES), 1)
    first_half = (lane % HEAD_DIM) < (HEAD_DIM // 2)
    hk = _dot(xb, wk_ref[...])
    hq = _dot(xb, wq_ref[...])
    for c in range(tn // LANES):
        cs = slice(c * LANES, (c + 1) * LANES)
        kr = _rope_tile(hk[:, cs], cos, sin, first_half)
        k_ref[:, cs] = kr
        kb_ref[:, cs] = kr.astype(BF16)
        qr = _rope_tile(hq[:, cs], cos, sin, first_half)
        qb_ref[:, cs] = (qr * ATTN_SCALE).astype(BF16)
    hv = _dot(xb, wv_ref[...])
    v_ref[...] = hv
    vb_ref[...] = hv.astype(BF16)


def _proj(x, wk, wv, wq, cos_t, sin_t, tm, tn):
    m, d = x.shape
    wspec = pl.BlockSpec((d, tn), lambda i, j: (0, j))
    tspec = pl.BlockSpec((tm, LANES), lambda i, j: (i, 0))
    ospec = pl.BlockSpec((tm, tn), lambda i, j: (i, j))
    return pl.pallas_call(
        _proj_kernel,
        grid=(m // tm, d // tn),
        in_specs=[pl.BlockSpec((tm, d), lambda i, j: (i, 0)), wspec, wspec, wspec, tspec, tspec],
        out_specs=[ospec] * 5,
        out_shape=[jax.ShapeDtypeStruct((m, d), F32), jax.ShapeDtypeStruct((m, d), F32),
                   jax.ShapeDtypeStruct((m, d), BF16), jax.ShapeDtypeStruct((m, d), BF16),
                   jax.ShapeDtypeStruct((m, d), BF16)],
        scratch_shapes=[pltpu.VMEM((tm, d), BF16)],
        compiler_params=_cparams("parallel", "arbitrary"),
        name="proj",
    )(x, wk, wv, wq, cos_t, sin_t)


def _diff_lambda(lam_ref, lam_init):
    lp = lam_ref[...]
    s1 = jnp.sum(lp[0:1, :] * lp[1:2, :], axis=-1, keepdims=True)
    s2 = jnp.sum(lp[2:3, :] * lp[3:4, :], axis=-1, keepdims=True)
    return jnp.exp(s1) - jnp.exp(s2) + lam_init


def _diff_finalize(o1, o2, lam, g, lam_init):
    o = o1 - lam * o2
    o = o * lax.rsqrt(jnp.mean(o * o, axis=-1, keepdims=True) + LN_EPS)
    return o * g * (1.0 - lam_init)


def _attn_prompt_kernel(q_ref, k_ref, v_ref, lam_ref, g_ref, o_ref, *, lam_init):
    tq = q_ref.shape[0]
    i = pl.program_id(1)
    q = q_ref[...]
    qs = (q[:, :HEAD_DIM], q[:, HEAD_DIM:])

    def step(kt, carry, masked):
        base = pl.multiple_of(kt * tq, tq)
        k = k_ref[pl.ds(base, tq), :]
        v = v_ref[pl.ds(base, tq), :]
        if masked:
            row = lax.broadcasted_iota(jnp.int32, (tq, tq), 0)
            col = lax.broadcasted_iota(jnp.int32, (tq, tq), 1)
            keep = col <= row
        out = []
        for mp in range(2):
            m_old, l_old, acc_old = carry[mp]
            s = lax.dot_general(qs[mp], k[:, mp * HEAD_DIM:(mp + 1) * HEAD_DIM],
                                (((1,), (1,)), ((), ())), preferred_element_type=F32)
            if masked:
                s = jnp.where(keep, s, -jnp.inf)
            m_new = jnp.maximum(m_old, jnp.max(s, axis=-1, keepdims=True))
            a = jnp.exp(m_old - m_new)
            p = jnp.exp(s - m_new)
            l_new = a * l_old + jnp.sum(p, axis=-1, keepdims=True)
            acc_new = a * acc_old + _dot(p.astype(BF16), v)
            out.append((m_new, l_new, acc_new))
        return tuple(out)

    init = tuple((jnp.full((tq, 1), -jnp.inf, F32), jnp.zeros((tq, 1), F32),
                  jnp.zeros((tq, V_DIM), F32)) for _ in range(2))
    carry = lax.fori_loop(0, i, lambda kt, c: step(kt, c, False), init)
    (_, l1, acc1), (_, l2, acc2) = step(i, carry, True)
    lam = _diff_lambda(lam_ref, lam_init)
    o_ref[...] = _diff_finalize(acc1 / l1, acc2 / l2, lam, g_ref[...], lam_init).astype(o_ref.dtype)


def _attn_prompt(qb, kb, vb, lam_p, sub_g, lam_init, tq):
    s, d = qb.shape
    nh = d // V_DIM
    return pl.pallas_call(
        functools.partial(_attn_prompt_kernel, lam_init=lam_init),
        grid=(nh, s // tq),
        in_specs=[
            pl.BlockSpec((tq, V_DIM), lambda h, i: (i, h)),
            pl.BlockSpec((s, V_DIM), lambda h, i: (0, h)),
            pl.BlockSpec((s, V_DIM), lambda h, i: (0, h)),
            pl.BlockSpec(lam_p.shape, lambda h, i: (0, 0)),
            pl.BlockSpec((1, V_DIM), lambda h, i: (0, 0)),
        ],
        out_specs=pl.BlockSpec((tq, V_DIM), lambda h, i: (i, h)),
        out_shape=jax.ShapeDtypeStruct((s, d), BF16),
        compiler_params=_cparams("parallel", "arbitrary"),
        name="attn_prompt",
    )(qb, kb, vb, lam_p, sub_g)


def _attn_sample_kernel(pt_ref, q_ref, kn_ref, vn_ref, lam_ref, g_ref, *refs,
                        lam_init, pages_per_step):
    k_refs = refs[:pages_per_step]
    v_refs = refs[pages_per_step:2 * pages_per_step]
    o_ref = refs[2 * pages_per_step]
    qb_ref, m_ref, l_ref, acc_ref = refs[2 * pages_per_step + 1:]
    step_idx = pl.program_id(1)
    n_maps, _, page = qb_ref.shape
    n_heads = n_maps // 2

    @pl.when(step_idx == 0)
    def _():
        q = q_ref[0]
        qb_ref[...] = jnp.broadcast_to(q, qb_ref.shape)
        s_self = jnp.sum(q * kn_ref[0], axis=1)
        m_ref[...] = s_self
        l_ref[...] = jnp.ones_like(l_ref)
        vn = vn_ref[0]
        for h in range(n_heads):
            acc_ref[2 * h:2 * h + 2, :] = jnp.broadcast_to(vn[h:h + 1, :], (2, V_DIM))

    row = lax.broadcasted_iota(jnp.int32, (n_maps, page), 0)
    for kp_ref, vp_ref in zip(k_refs, v_refs):
        s = jnp.sum(kp_ref[0] * qb_ref[...], axis=1)
        m_old = m_ref[...]
        m_new = jnp.maximum(m_old, jnp.max(s, axis=-1, keepdims=True))
        a = jnp.exp(m_old - m_new)
        p = jnp.exp(s - m_new)
        l_ref[...] = a * l_ref[...] + jnp.sum(p, axis=-1, keepdims=True)
        m_ref[...] = m_new
        pv = jnp.zeros((n_maps, V_DIM), F32)
        for h in range(n_heads):
            ph = jnp.where((row >> 1) == h, p, 0.0).astype(BF16)
            pv = pv + _dot(ph, vp_ref[0, :, h, :].astype(BF16))
        acc_ref[...] = a * acc_ref[...] + pv

    @pl.when(step_idx == pl.num_programs(1) - 1)
    def _():
        acc_ref[...] = acc_ref[...] / l_ref[...]
        o1 = acc_ref[pl.ds(0, n_heads, stride=2), :]
        o2 = acc_ref[pl.ds(1, n_heads, stride=2), :]
        lam = _diff_lambda(lam_ref, lam_init)
        o_ref[0] = _diff_finalize(o1, o2, lam, g_ref[...], lam_init).astype(o_ref.dtype)


def _attn_sample(page_table, q4, kn4, vn3, lam_p, sub_g, cache_kt, cache_v, lam_init, pages_per_step):
    nb, n_pages = page_table.shape
    _, n_maps, hd, page = cache_kt.shape
    n_heads = cache_v.shape[2]
    pps = pages_per_step

    def kspec(j):
        return pl.BlockSpec((1, n_maps, hd, page), lambda b, s, pt: (pt[b, s * pps + j], 0, 0, 0))

    def vspec(j):
        return pl.BlockSpec((1, page, n_heads, V_DIM), lambda b, s, pt: (pt[b, s * pps + j], 0, 0, 0))

    grid_spec = pltpu.PrefetchScalarGridSpec(
        num_scalar_prefetch=1,
        grid=(nb, n_pages // pps),
        in_specs=[
            pl.BlockSpec((1, n_maps, hd, 1), lambda b, s, pt: (b, 0, 0, 0)),
            pl.BlockSpec((1, n_maps, hd, 1), lambda b, s, pt: (b, 0, 0, 0)),
            pl.BlockSpec((1, n_heads, V_DIM), lambda b, s, pt: (b, 0, 0)),
            pl.BlockSpec(lam_p.shape, lambda b, s, pt: (0, 0)),
            pl.BlockSpec((1, V_DIM), lambda b, s, pt: (0, 0)),
        ] + [kspec(j) for j in range(pps)] + [vspec(j) for j in range(pps)],
        out_specs=pl.BlockSpec((1, n_heads, V_DIM), lambda b, s, pt: (b, 0, 0)),
        scratch_shapes=[
            pltpu.VMEM((n_maps, hd, page), F32),
            pltpu.VMEM((n_maps, 1), F32),
            pltpu.VMEM((n_maps, 1), F32),
            pltpu.VMEM((n_maps, V_DIM), F32),
        ],
    )
    return pl.pallas_call(
        functools.partial(_attn_sample_kernel, lam_init=lam_init, pages_per_step=pps),
        grid_spec=grid_spec,
        out_shape=jax.ShapeDtypeStruct((nb, n_heads, V_DIM), BF16),
        compiler_params=_cparams("parallel", "arbitrary"),
        name="attn_sample",
    )(page_table, q4, kn4, vn3, lam_p, sub_g, *([cache_kt] * pps), *([cache_v] * pps))


def _wo_kernel(a_ref, x_ref, wo_ref, pg_ref, pb_ref, o_ref, *, alpha):
    m = _dot(a_ref[...], wo_ref[...])
    o_ref[...] = _ln(alpha * x_ref[...] + m, pg_ref[...], pb_ref[...])


def _wo(a, x, wo, pg, pb, alpha, tm):
    m, d = x.shape
    vec = pl.BlockSpec((1, d), lambda i: (0, 0))
    return pl.pallas_call(
        functools.partial(_wo_kernel, alpha=alpha),
        grid=(m // tm,),
        in_specs=[
            pl.BlockSpec((tm, d), lambda i: (i, 0)),
            pl.BlockSpec((tm, d), lambda i: (i, 0)),
            pl.BlockSpec((d, d), lambda i: (0, 0)),
            vec, vec,
        ],
        out_specs=pl.BlockSpec((tm, d), lambda i: (i, 0)),
        out_shape=jax.ShapeDtypeStruct((m, d), F32),
        compiler_params=_cparams("parallel"),
        name="wo",
    )(a, x, wo, pg, pb)


def _moe_kernel(x_ref, wr_ref, wg_ref, wu_ref, wd_ref, pg_ref, pb_ref, o_ref, xb_ref, comb_ref,
                *, alpha):
    e = pl.program_id(1)
    n_exp = wr_ref.shape[1]

    @pl.when(e == 0)
    def _():
        x = x_ref[...]
        xb_ref[...] = x.astype(BF16)
        logits = jnp.dot(x, wr_ref[...], preferred_element_type=F32,
                         precision=lax.Precision.HIGHEST)
        idx = lax.broadcasted_iota(jnp.int32, logits.shape, 1)
        v1 = jnp.max(logits, axis=-1, keepdims=True)
        i1 = jnp.min(jnp.where(logits == v1, idx, n_exp), axis=-1, keepdims=True)
        rest = jnp.where(idx == i1, -jnp.inf, logits)
        v2 = jnp.max(rest, axis=-1, keepdims=True)
        i2 = jnp.min(jnp.where(rest == v2, idx, n_exp), axis=-1, keepdims=True)
        t = jnp.exp(v2 - v1)
        g1 = 1.0 / (1.0 + t)
        g2 = t / (1.0 + t)
        comb_ref[...] = jnp.where(idx == i1, g1, 0.0) + jnp.where(idx == i2, g2, 0.0)

    xb = xb_ref[...]
    comb = comb_ref[...]
    idx = lax.broadcasted_iota(jnp.int32, comb.shape, 1)
    w = jnp.sum(jnp.where(idx == e, comb, 0.0), axis=-1, keepdims=True)
    h = _silu(_dot(xb, wg_ref[0])) * _dot(xb, wu_ref[0])
    part = _dot((h * w).astype(BF16), wd_ref[0])

    @pl.when(e == 0)
    def _():
        o_ref[...] = part

    @pl.when(e > 0)
    def _():
        o_ref[...] += part

    @pl.when(e == pl.num_programs(1) - 1)
    def _():
        o_ref[...] = _ln(alpha * x_ref[...] + o_ref[...], pg_ref[...], pb_ref[...])


def _moe(x, wr, wg, wu, wd, pg, pb, alpha, tm):
    m, d = x.shape
    n_exp, _, dexp = wg.shape
    vec = pl.BlockSpec((1, d), lambda i, e: (0, 0))
    return pl.pallas_call(
        functools.partial(_moe_kernel, alpha=alpha),
        grid=(m // tm, n_exp),
        in_specs=[
            pl.BlockSpec((tm, d), lambda i, e: (i, 0)),
            pl.BlockSpec((d, n_exp), lambda i, e: (0, 0)),
            pl.BlockSpec((1, d, dexp), lambda i, e: (e, 0, 0)),
            pl.BlockSpec((1, d, dexp), lambda i, e: (e, 0, 0)),
            pl.BlockSpec((1, dexp, d), lambda i, e: (e, 0, 0)),
            vec, vec,
        ],
        out_specs=pl.BlockSpec((tm, d), lambda i, e: (i, 0)),
        out_shape=jax.ShapeDtypeStruct((m, d), F32),
        scratch_shapes=[pltpu.VMEM((tm, d), BF16), pltpu.VMEM((tm, n_exp), F32)],
        compiler_params=_cparams("parallel", "arbitrary"),
        name="moe",
    )(x, wr, wg, wu, wd, pg, pb)


def kernel(x_prompt, x_sample, state_conv, cache_k, cache_v, page_table, p_prompt, p_sample,
           conv_w_pw1, conv_b_pw1, conv_w_dw, conv_b_dw, conv_ln_g, conv_ln_b, conv_w_pw2,
           kv_w_k, kv_w_v, attn_w_q, attn_lambda, attn_sub_g, attn_w_o,
           ffn_w_gate, ffn_w_up, ffn_w_down, moe_w_router, moe_w_gate, moe_w_up, moe_w_down,
           post_ln_g, post_ln_b, ple_w_proj, ple_w_gate):
    bp, sp, d = x_prompt.shape
    bd, sd, _ = x_sample.shape
    depth = post_ln_g.shape[0]
    n_a = conv_w_pw1.shape[0]
    assert bp == 1 and sd == 1 and depth == 2 and n_a == 1
    n_heads = d // V_DIM
    n_maps = 2 * n_heads
    past = page_table.shape[1] * cache_k.shape[1]
    alpha = (2.0 * depth) ** 0.25
    lam_init = 0.8 - 0.6 * math.exp(-0.3 * 1)

    bf = lambda w: w.astype(BF16)
    row = lambda v: v.reshape(1, -1)

    xp = x_prompt.reshape(sp, d)
    xd = x_sample.reshape(bd, d)
    pp = p_prompt.reshape(depth, sp, -1)
    pdm = p_sample.reshape(depth, bd, -1)

    w1, w2 = bf(conv_w_pw1[0]), bf(conv_w_pw2[0])
    b1 = row(conv_b_pw1[0])
    conv_args = (conv_w_dw[0], row(conv_b_dw[0]), row(conv_ln_g[0]), row(conv_ln_b[0]), w2,
                 row(post_ln_g[0, 0]), row(post_ln_b[0, 0]), alpha)
    up = _glu(xp, w1, b1, 512, 512)
    ud = _glu(xd, w1, b1, bd, 512)
    conv_prompt = up[sp - (CONV_W - 1):].reshape(1, 1, CONV_W - 1, d)
    hist = jnp.transpose(state_conv[0], (1, 0, 2))
    conv_sample = jnp.transpose(jnp.concatenate([hist[1:], ud[None]], axis=0), (1, 0, 2))[None]
    xp = _conv_prompt(up, xp, *conv_args, 256)
    xd = _conv_sample(hist, ud, xd, *conv_args)

    fw = (bf(ffn_w_gate[0]), bf(ffn_w_up[0]), bf(ffn_w_down[0]), row(post_ln_g[0, 1]), row(post_ln_b[0, 1]), alpha)
    xp = _swiglu(xp, *fw, 512, 512)
    xd = _swiglu(xd, *fw, bd, 512)

    pw = (bf(ple_w_gate[0]), bf(ple_w_proj[0]))
    xp = _ple(xp, pp[0], *pw, 512, 512)
    xd = _ple(xd, pdm[0], *pw, bd, 512)

    wk, wv, wq = bf(kv_w_k), bf(kv_w_v), bf(attn_w_q[0])
    cos_p, sin_p = _rope_tables(jnp.arange(sp))
    cos_d, sin_d = _rope_tables(jnp.full((bd,), past))
    k_p, v_p, kb_p, vb_p, qb_p = _proj(xp, wk, wv, wq, cos_p, sin_p, 512, 512)
    k_d, v_d, _, _, qb_d = _proj(xd, wk, wv, wq, cos_d, sin_d, bd, 512)

    lam_p = attn_lambda[0]
    sub_g = row(attn_sub_g[0])
    ap = _attn_prompt(qb_p, kb_p, vb_p, lam_p, sub_g, lam_init, 256)
    cache_kt = jnp.transpose(cache_k, (0, 2, 3, 1))
    q4 = qb_d.astype(F32).reshape(bd, n_maps, HEAD_DIM, 1)
    kn4 = k_d.astype(BF16).astype(F32).reshape(bd, n_maps, HEAD_DIM, 1)
    ad = _attn_sample(page_table, q4, kn4, v_d.reshape(bd, n_heads, V_DIM), lam_p, sub_g,
                      cache_kt, cache_v, lam_init, 4).reshape(bd, d)
    wo_args = (bf(attn_w_o[0]), row(post_ln_g[1, 0]), row(post_ln_b[1, 0]), alpha)
    xp = _wo(ap, xp, *wo_args, 256)
    xd = _wo(ad, xd, *wo_args, bd)

    mw = (moe_w_router[0], bf(moe_w_gate[0]), bf(moe_w_up[0]), bf(moe_w_down[0]),
          row(post_ln_g[1, 1]), row(post_ln_b[1, 1]), alpha)
    xp = _moe(xp, *mw, 256)
    xd = _moe(xd, *mw, bd)

    pw = (bf(ple_w_gate[1]), bf(ple_w_proj[1]))
    xp = _ple(xp, pp[1], *pw, 512, 512)
    xd = _ple(xd, pdm[1], *pw, bd, 512)

    return (xp.reshape(bp, sp, d), xd.reshape(bd, sd, d), conv_prompt, conv_sample,
            k_p.reshape(bp, sp, n_maps, HEAD_DIM), v_p.reshape(bp, sp, n_heads, V_DIM),
            k_d.reshape(bd, sd, n_maps, HEAD_DIM), v_d.reshape(bd, sd, n_heads, V_DIM))
```

```python
import functools
import math

import jax
import jax.numpy as jnp
from jax import lax
from jax.experimental import pallas as pl
from jax.experimental.pallas import tpu as pltpu

HEAD_DIM = 64
V_DIM = 2 * HEAD_DIM
CONV_W = 31
TOP_K = 2
ROPE_THETA = 10000.0
LN_EPS = 1e-5
ATTN_SCALE = HEAD_DIM ** -0.5

LANES = 128
SUBLANES = 8
HALO = 32
VMEM_LIMIT = 56 * 1024 * 1024

F32 = jnp.float32
BF16 = jnp.bfloat16


def _cparams(*sem):
    return pltpu.CompilerParams(dimension_semantics=sem, vmem_limit_bytes=VMEM_LIMIT)


def _dot(a, b):
    return jnp.dot(a, b, preferred_element_type=F32)


def _ln(x, g, b):
    mu = jnp.mean(x, axis=-1, keepdims=True)
    xc = x - mu
    var = jnp.mean(xc * xc, axis=-1, keepdims=True)
    return xc * lax.rsqrt(var + LN_EPS) * g + b


def _sigmoid(x):
    return 1.0 / (1.0 + jnp.exp(-x))


def _silu(x):
    return x * _sigmoid(x)


def _glu_kernel(x_ref, wa_ref, wg_ref, ba_ref, bg_ref, u_ref, xb_ref):
    @pl.when(pl.program_id(1) == 0)
    def _():
        xb_ref[...] = x_ref[...].astype(BF16)

    xb = xb_ref[...]
    a = _dot(xb, wa_ref[...]) + ba_ref[...]
    g = _dot(xb, wg_ref[...]) + bg_ref[...]
    u_ref[...] = a * _sigmoid(g)


def _glu(x, w1, b1, tm, tn):
    m, d = x.shape
    nj = d // tn
    return pl.pallas_call(
        _glu_kernel,
        grid=(m // tm, nj),
        in_specs=[
            pl.BlockSpec((tm, d), lambda i, j: (i, 0)),
            pl.BlockSpec((d, tn), lambda i, j: (0, j)),
            pl.BlockSpec((d, tn), lambda i, j: (0, j + nj)),
            pl.BlockSpec((1, tn), lambda i, j: (0, j)),
            pl.BlockSpec((1, tn), lambda i, j: (0, j + nj)),
        ],
        out_specs=pl.BlockSpec((tm, tn), lambda i, j: (i, j)),
        out_shape=jax.ShapeDtypeStruct((m, d), F32),
        scratch_shapes=[pltpu.VMEM((tm, d), BF16)],
        compiler_params=_cparams("parallel", "arbitrary"),
        name="glu",
    )(x, w1, w1, b1, b1)


def _mix_tail(y, x, g_ref, b_ref, w2_ref, pg_ref, pb_ref, alpha):
    y = _silu(_ln(y, g_ref[...], b_ref[...]))
    m = _dot(y.astype(BF16), w2_ref[...])
    return _ln(alpha * x + m, pg_ref[...], pb_ref[...])


def _conv_prompt_kernel(uprev_ref, u_ref, x_ref, wdw_ref, bdw_ref, g_ref, b_ref, w2_ref,
                        pg_ref, pb_ref, o_ref, full_ref, y_ref, *, alpha, rows):
    tm, d = u_ref.shape
    first = pl.program_id(0) == 0
    full_ref[0:HALO, :] = jnp.where(first, 0.0, uprev_ref[...])
    full_ref[HALO:, :] = u_ref[...]
    off = HALO - (CONV_W - 1)

    def col_body(c, carry):
        cs = pl.ds(pl.multiple_of(c * LANES, LANES), LANES)
        w = wdw_ref[:, cs]
        bias = bdw_ref[:, cs]

        def row_body(r, carry2):
            base = pl.multiple_of(r * rows, rows)
            win = full_ref[pl.ds(base, rows + HALO), cs]
            acc = jnp.zeros((rows, LANES), F32)
            for res in range(SUBLANES):
                sh = win if res == 0 else pltpu.roll(win, rows + HALO - res, 0)
                for t in range(CONV_W):
                    if (off + t) % SUBLANES == res:
                        a8 = off + t - res
                        acc = acc + sh[a8:a8 + rows, :] * w[t:t + 1, :]
            y_ref[pl.ds(base, rows), cs] = acc + bias
            return carry2

        return lax.fori_loop(0, tm // rows, row_body, carry)

    lax.fori_loop(0, d // LANES, col_body, 0)
    o_ref[...] = _mix_tail(y_ref[...], x_ref[...], g_ref, b_ref, w2_ref, pg_ref, pb_ref, alpha)


def _conv_prompt(u, x, wdw, bdw, g, b, w2, pg, pb, alpha, tm):
    m, d = x.shape
    nh = tm // HALO
    vec = pl.BlockSpec((1, d), lambda i: (0, 0))
    return pl.pallas_call(
        functools.partial(_conv_prompt_kernel, alpha=alpha, rows=32),
        grid=(m // tm,),
        in_specs=[
            pl.BlockSpec((HALO, d), lambda i: (jnp.maximum(i * nh - 1, 0), 0)),
            pl.BlockSpec((tm, d), lambda i: (i, 0)),
            pl.BlockSpec((tm, d), lambda i: (i, 0)),
            pl.BlockSpec((CONV_W, d), lambda i: (0, 0)),
            vec, vec, vec,
            pl.BlockSpec((d, d), lambda i: (0, 0)),
            vec, vec,
        ],
        out_specs=pl.BlockSpec((tm, d), lambda i: (i, 0)),
        out_shape=jax.ShapeDtypeStruct((m, d), F32),
        scratch_shapes=[pltpu.VMEM((tm + HALO, d), F32), pltpu.VMEM((tm, d), F32)],
        compiler_params=_cparams("parallel"),
        name="conv_prompt",
    )(u, u, x, wdw, bdw, g, b, w2, pg, pb)


def _conv_sample_kernel(hist_ref, u_ref, x_ref, wdw_ref, bdw_ref, g_ref, b_ref, w2_ref,
                        pg_ref, pb_ref, o_ref, *, alpha):
    y = u_ref[...] * wdw_ref[CONV_W - 1:CONV_W, :]
    for t in range(CONV_W - 1):
        y = y + hist_ref[t] * wdw_ref[t:t + 1, :]
    y = y + bdw_ref[...]
    o_ref[...] = _mix_tail(y, x_ref[...], g_ref, b_ref, w2_ref, pg_ref, pb_ref, alpha)


def _conv_sample(hist, u, x, wdw, bdw, g, b, w2, pg, pb, alpha):
    m, d = x.shape
    return pl.pallas_call(
        functools.partial(_conv_sample_kernel, alpha=alpha),
        out_shape=jax.ShapeDtypeStruct((m, d), F32),
        compiler_params=pltpu.CompilerParams(vmem_limit_bytes=VMEM_LIMIT),
        name="conv_sample",
    )(hist, u, x, wdw, bdw, g, b, w2, pg, pb)


def _swiglu_kernel(x_ref, wg_ref, wu_ref, wd_ref, pg_ref, pb_ref, o_ref, xb_ref, *, alpha):
    f = pl.program_id(1)

    @pl.when(f == 0)
    def _():
        xb_ref[...] = x_ref[...].astype(BF16)

    xb = xb_ref[...]
    h = _silu(_dot(xb, wg_ref[...])) * _dot(xb, wu_ref[...])
    part = _dot(h.astype(BF16), wd_ref[...])

    @pl.when(f == 0)
    def _():
        o_ref[...] = part

    @pl.when(f > 0)
    def _():
        o_ref[...] += part

    @pl.when(f == pl.num_programs(1) - 1)
    def _():
        o_ref[...] = _ln(alpha * x_ref[...] + o_ref[...], pg_ref[...], pb_ref[...])


def _swiglu(x, wg, wu, wd, pg, pb, alpha, tm, tf):
    m, d = x.shape
    dff = wg.shape[1]
    vec = pl.BlockSpec((1, d), lambda i, f: (0, 0))
    return pl.pallas_call(
        functools.partial(_swiglu_kernel, alpha=alpha),
        grid=(m // tm, dff // tf),
        in_specs=[
            pl.BlockSpec((tm, d), lambda i, f: (i, 0)),
            pl.BlockSpec((d, tf), lambda i, f: (0, f)),
            pl.BlockSpec((d, tf), lambda i, f: (0, f)),
            pl.BlockSpec((tf, d), lambda i, f: (f, 0)),
            vec, vec,
        ],
        out_specs=pl.BlockSpec((tm, d), lambda i, f: (i, 0)),
        out_shape=jax.ShapeDtypeStruct((m, d), F32),
        scratch_shapes=[pltpu.VMEM((tm, d), BF16)],
        compiler_params=_cparams("parallel", "arbitrary"),
        name="swiglu",
    )(x, wg, wu, wd, pg, pb)


def _ple_kernel(x_ref, xc_ref, p_ref, wg_ref, wp_ref, o_ref, xb_ref):
    @pl.when(pl.program_id(1) == 0)
    def _():
        xb_ref[...] = x_ref[...].astype(BF16)

    gate = _sigmoid(_dot(xb_ref[...], wg_ref[...]))
    o_ref[...] = xc_ref[...] + gate * _dot(p_ref[...].astype(BF16), wp_ref[...])


def _ple(x, p, wg, wp, tm, tn):
    m, d = x.shape
    pd = p.shape[1]
    return pl.pallas_call(
        _ple_kernel,
        grid=(m // tm, d // tn),
        in_specs=[
            pl.BlockSpec((tm, d), lambda i, j: (i, 0)),
            pl.BlockSpec((tm, tn), lambda i, j: (i, j)),
            pl.BlockSpec((tm, pd), lambda i, j: (i, 0)),
            pl.BlockSpec((d, tn), lambda i, j: (0, j)),
            pl.BlockSpec((pd, tn), lambda i, j: (0, j)),
        ],
        out_specs=pl.BlockSpec((tm, tn), lambda i, j: (i, j)),
        out_shape=jax.ShapeDtypeStruct((m, d), F32),
        scratch_shapes=[pltpu.VMEM((tm, d), BF16)],
        compiler_params=_cparams("parallel", "arbitrary"),
        name="ple",
    )(x, x, p, wg, wp)


def _rope_tables(pos):
    half = HEAD_DIM // 2
    inv = ROPE_THETA ** (-jnp.arange(half, dtype=F32) / half)
    ang = pos.astype(F32)[:, None] * inv[None, :]
    cos, sin = jnp.cos(ang), jnp.sin(ang)
    cos_t = jnp.concatenate([cos, cos, cos, cos], axis=-1)
    sin_t = jnp.concatenate([-sin, sin, -sin, sin], axis=-1)
    return cos_t, sin_t


def _rope_tile(h, cos, sin, first_half):
    partner = jnp.where(first_half, pltpu.roll(h, LANES - HEAD_DIM // 2, 1),
                        pltpu.roll(h, HEAD_DIM // 2, 1))
    return h * cos + partner * sin


def _proj_kernel(x_ref, wk_ref, wv_ref, wq_ref, cos_ref, sin_ref, *refs, transposed):
    if transposed:
        k_ref, v_ref, kb_ref, qt_ref, vt_ref, xb_ref = refs
    else:
        k_ref, v_ref, qb_ref, xb_ref = refs

    @pl.when(pl.program_id(1) == 0)
    def _():
        xb_ref[...] = x_ref[...].astype(BF16)

    xb = xb_ref[...]
    tm, tn = k_ref.shape
    cos, sin = cos_ref[...], sin_ref[...]
    lane = lax.broadcasted_iota(jnp.int32, (tm, LANES), 1)
    first_half = (lane % HEAD_DIM) < (HEAD_DIM // 2)
    hk = _dot(xb, wk_ref[...])
    hq = _dot(xb, wq_ref[...])
    for c in range(tn // LANES):
        cs = slice(c * LANES, (c + 1) * LANES)
        kr = _rope_tile(hk[:, cs], cos, sin, first_half)
        k_ref[:, cs] = kr
        qr = _rope_tile(hq[:, cs], cos, sin, first_half) * ATTN_SCALE
        if transposed:
            kb_ref[:, cs] = kr.astype(BF16)
            qt_ref[cs, :] = qr.T.astype(BF16)
        else:
            qb_ref[:, cs] = qr.astype(BF16)
    hv = _dot(xb, wv_ref[...])
    v_ref[...] = hv
    if transposed:
        vt_ref[...] = hv.T.astype(BF16)


def _proj(x, wk, wv, wq, cos_t, sin_t, tm, tn, transposed):
    m, d = x.shape
    wspec = pl.BlockSpec((d, tn), lambda i, j: (0, j))
    tspec = pl.BlockSpec((tm, LANES), lambda i, j: (i, 0))
    ospec = pl.BlockSpec((tm, tn), lambda i, j: (i, j))
    tr_spec = pl.BlockSpec((tn, tm), lambda i, j: (j, i))
    f32_out = jax.ShapeDtypeStruct((m, d), F32)
    if transposed:
        out_specs = [ospec, ospec, ospec, tr_spec, tr_spec]
        out_shape = [f32_out, f32_out, jax.ShapeDtypeStruct((m, d), BF16),
                     jax.ShapeDtypeStruct((d, m), BF16), jax.ShapeDtypeStruct((d, m), BF16)]
    else:
        out_specs = [ospec, ospec, ospec]
        out_shape = [f32_out, f32_out, jax.ShapeDtypeStruct((m, d), BF16)]
    return pl.pallas_call(
        functools.partial(_proj_kernel, transposed=transposed),
        grid=(m // tm, d // tn),
        in_specs=[pl.BlockSpec((tm, d), lambda i, j: (i, 0)), wspec, wspec, wspec, tspec, tspec],
        out_specs=out_specs,
        out_shape=out_shape,
        scratch_shapes=[pltpu.VMEM((tm, d), BF16)],
        compiler_params=_cparams("parallel", "arbitrary"),
        name="proj",
    )(x, wk, wv, wq, cos_t, sin_t)


def _diff_lambda(lam_ref, lam_init):
    lp = lam_ref[...]
    s1 = jnp.sum(lp[0:1, :] * lp[1:2, :], axis=-1, keepdims=True)
    s2 = jnp.sum(lp[2:3, :] * lp[3:4, :], axis=-1, keepdims=True)
    return jnp.exp(s1) - jnp.exp(s2) + lam_init


def _diff_finalize(o1, o2, lam, g, lam_init):
    o = o1 - lam * o2
    o = o * lax.rsqrt(jnp.mean(o * o, axis=-1, keepdims=True) + LN_EPS)
    return o * g * (1.0 - lam_init)


def _attn_prompt_kernel(qt_ref, k_ref, vt_ref, lam_ref, g_ref, o_ref,
                        sa_ref, sb_ref, m_ref, l_ref, acc_ref, *, lam_init, tk):
    tq = qt_ref.shape[1]
    i = pl.program_id(1)
    q0 = i * tq
    n_full = q0 // tk
    qt = qt_ref[...]
    feat = lax.broadcasted_iota(jnp.int32, qt.shape, 0)
    qz = jnp.concatenate([jnp.where(feat < HEAD_DIM, qt, 0), jnp.where(feat >= HEAD_DIM, qt, 0)], axis=1)
    m_ref[...] = jnp.full(m_ref.shape, -jnp.inf, F32)
    l_ref[...] = jnp.zeros(l_ref.shape, F32)
    acc_ref[...] = jnp.zeros(acc_ref.shape, F32)

    def scores(b, dst_ref):
        base = pl.multiple_of(b * tk, tk)
        dst_ref[...] = _dot(k_ref[pl.ds(base, tk), :], qz)

    def softmax_pv(b, src_ref, masked):
        base = pl.multiple_of(b * tk, tk)
        s = src_ref[...]
        if masked:
            key = base + lax.broadcasted_iota(jnp.int32, s.shape, 0)
            qry = q0 + lax.broadcasted_iota(jnp.int32, s.shape, 1) % tq
            s = jnp.where(key <= qry, s, -jnp.inf)
        m_old = m_ref[...]
        m_new = jnp.maximum(m_old, jnp.max(s, axis=0, keepdims=True))
        a = jnp.exp(m_old - m_new)
        p = jnp.exp(s - m_new)
        l_ref[...] = a * l_ref[...] + jnp.sum(p, axis=0, keepdims=True)
        m_ref[...] = m_new
        acc_ref[...] = a * acc_ref[...] + _dot(vt_ref[:, pl.ds(base, tk)], p.astype(BF16))

    scores(0, sa_ref)

    def pair(t, carry):
        b = 2 * t
        scores(b + 1, sb_ref)
        softmax_pv(b, sa_ref, False)
        scores(b + 2, sa_ref)
        softmax_pv(b + 1, sb_ref, False)
        return carry

    lax.fori_loop(0, n_full // 2, pair, 0)

    @pl.when(n_full % 2 == 1)
    def _():
        scores(n_full, sb_ref)
        softmax_pv(n_full - 1, sa_ref, False)
        softmax_pv(n_full, sb_ref, True)

    @pl.when(n_full % 2 == 0)
    def _():
        softmax_pv(n_full, sa_ref, True)

    lam = _diff_lambda(lam_ref, lam_init)
    o = acc_ref[...] * (1.0 / l_ref[...])
    ot = o[:, :tq] - lam * o[:, tq:]
    ot = ot * lax.rsqrt(jnp.mean(ot * ot, axis=0, keepdims=True) + LN_EPS)
    ot = ot * g_ref[...] * (1.0 - lam_init)
    o_ref[...] = ot.T.astype(o_ref.dtype)


def _attn_prompt(qt, kb, vt, lam_p, sub_g_col, lam_init, tq, tk):
    d, s = qt.shape
    nh = d // V_DIM
    return pl.pallas_call(
        functools.partial(_attn_prompt_kernel, lam_init=lam_init, tk=tk),
        grid=(nh, s // tq),
        in_specs=[
            pl.BlockSpec((V_DIM, tq), lambda h, i: (h, i)),
            pl.BlockSpec((s, V_DIM), lambda h, i: (0, h)),
            pl.BlockSpec((V_DIM, s), lambda h, i: (h, 0)),
            pl.BlockSpec(lam_p.shape, lambda h, i: (0, 0)),
            pl.BlockSpec((V_DIM, 1), lambda h, i: (0, 0)),
        ],
        out_specs=pl.BlockSpec((tq, V_DIM), lambda h, i: (i, h)),
        out_shape=jax.ShapeDtypeStruct((s, d), BF16),
        scratch_shapes=[pltpu.VMEM((tk, 2 * tq), F32), pltpu.VMEM((tk, 2 * tq), F32),
                        pltpu.VMEM((1, 2 * tq), F32), pltpu.VMEM((1, 2 * tq), F32),
                        pltpu.VMEM((V_DIM, 2 * tq), F32)],
        compiler_params=_cparams("parallel", "arbitrary"),
        name="attn_prompt",
    )(qt, kb, vt, lam_p, sub_g_col)


def _attn_sample_kernel(pt_ref, q_ref, kn_ref, vn_ref, lam_ref, g_ref, *refs,
                        lam_init, pages_per_step):
    k_refs = refs[:pages_per_step]
    v_refs = refs[pages_per_step:2 * pages_per_step]
    o_ref = refs[2 * pages_per_step]
    qb_ref, m_ref, l_ref, acc_ref = refs[2 * pages_per_step + 1:]
    step_idx = pl.program_id(1)
    n_maps, _, page = qb_ref.shape
    n_heads = n_maps // 2

    @pl.when(step_idx == 0)
    def _():
        q = q_ref[0]
        qb_ref[...] = jnp.broadcast_to(q, qb_ref.shape)
        s_self = jnp.sum(q * kn_ref[0], axis=1)
        m_ref[...] = s_self
        l_ref[...] = jnp.ones_like(l_ref)
        vn = vn_ref[0]
        for h in range(n_heads):
            acc_ref[2 * h:2 * h + 2, :] = jnp.broadcast_to(vn[h:h + 1, :], (2, V_DIM))

    row = lax.broadcasted_iota(jnp.int32, (n_maps, page), 0)
    for kp_ref, vp_ref in zip(k_refs, v_refs):
        s = jnp.sum(kp_ref[0] * qb_ref[...], axis=1)
        m_old = m_ref[...]
        m_new = jnp.maximum(m_old, jnp.max(s, axis=-1, keepdims=True))
        a = jnp.exp(m_old - m_new)
        p = jnp.exp(s - m_new)
        l_ref[...] = a * l_ref[...] + jnp.sum(p, axis=-1, keepdims=True)
        m_ref[...] = m_new
        pv = jnp.zeros((n_maps, V_DIM), F32)
        for h in range(n_heads):
            ph = jnp.where((row >> 1) == h, p, 0.0).astype(BF16)
            pv = pv + _dot(ph, vp_ref[0, :, h, :].astype(BF16))
        acc_ref[...] = a * acc_ref[...] + pv

    @pl.when(step_idx == pl.num_programs(1) - 1)
    def _():
        acc_ref[...] = acc_ref[...] / l_ref[...]
        o1 = acc_ref[pl.ds(0, n_heads, stride=2), :]
        o2 = acc_ref[pl.ds(1, n_heads, stride=2), :]
        lam = _diff_lambda(lam_ref, lam_init)
        o_ref[0] = _diff_finalize(o1, o2, lam, g_ref[...], lam_init).astype(o_ref.dtype)


def _attn_sample(page_table, q4, kn4, vn3, lam_p, sub_g, cache_kt, cache_v, lam_init, pages_per_step):
    nb, n_pages = page_table.shape
    _, n_maps, hd, page = cache_kt.shape
    n_heads = cache_v.shape[2]
    pps = pages_per_step

    def kspec(j):
        return pl.BlockSpec((1, n_maps, hd, page), lambda b, s, pt: (pt[b, s * pps + j], 0, 0, 0))

    def vspec(j):
        return pl.BlockSpec((1, page, n_heads, V_DIM), lambda b, s, pt: (pt[b, s * pps + j], 0, 0, 0))

    grid_spec = pltpu.PrefetchScalarGridSpec(
        num_scalar_prefetch=1,
        grid=(nb, n_pages // pps),
        in_specs=[
            pl.BlockSpec((1, n_maps, hd, 1), lambda b, s, pt: (b, 0, 0, 0)),
            pl.BlockSpec((1, n_maps, hd, 1), lambda b, s, pt: (b, 0, 0, 0)),
            pl.BlockSpec((1, n_heads, V_DIM), lambda b, s, pt: (b, 0, 0)),
            pl.BlockSpec(lam_p.shape, lambda b, s, pt: (0, 0)),
            pl.BlockSpec((1, V_DIM), lambda b, s, pt: (0, 0)),
        ] + [kspec(j) for j in range(pps)] + [vspec(j) for j in range(pps)],
        out_specs=pl.BlockSpec((1, n_heads, V_DIM), lambda b, s, pt: (b, 0, 0)),
        scratch_shapes=[
            pltpu.VMEM((n_maps, hd, page), F32),
            pltpu.VMEM((n_maps, 1), F32),
            pltpu.VMEM((n_maps, 1), F32),
            pltpu.VMEM((n_maps, V_DIM), F32),
        ],
    )
    return pl.pallas_call(
        functools.partial(_attn_sample_kernel, lam_init=lam_init, pages_per_step=pps),
        grid_spec=grid_spec,
        out_shape=jax.ShapeDtypeStruct((nb, n_heads, V_DIM), BF16),
        compiler_params=_cparams("parallel", "arbitrary"),
        name="attn_sample",
    )(page_table, q4, kn4, vn3, lam_p, sub_g, *([cache_kt] * pps), *([cache_v] * pps))


def _wo_kernel(a_ref, x_ref, wo_ref, pg_ref, pb_ref, o_ref, *, alpha):
    m = _dot(a_ref[...], wo_ref[...])
    o_ref[...] = _ln(alpha * x_ref[...] + m, pg_ref[...], pb_ref[...])


def _wo(a, x, wo, pg, pb, alpha, tm):
    m, d = x.shape
    vec = pl.BlockSpec((1, d), lambda i: (0, 0))
    return pl.pallas_call(
        functools.partial(_wo_kernel, alpha=alpha),
        grid=(m // tm,),
        in_specs=[
            pl.BlockSpec((tm, d), lambda i: (i, 0)),
            pl.BlockSpec((tm, d), lambda i: (i, 0)),
            pl.BlockSpec((d, d), lambda i: (0, 0)),
            vec, vec,
        ],
        out_specs=pl.BlockSpec((tm, d), lambda i: (i, 0)),
        out_shape=jax.ShapeDtypeStruct((m, d), F32),
        compiler_params=_cparams("parallel"),
        name="wo",
    )(a, x, wo, pg, pb)


def _moe_kernel(x_ref, wr_ref, wg_ref, wu_ref, wd_ref, pg_ref, pb_ref, o_ref, xb_ref, comb_ref,
                *, alpha):
    e = pl.program_id(1)
    n_exp = wr_ref.shape[1]

    @pl.when(e == 0)
    def _():
        x = x_ref[...]
        xb_ref[...] = x.astype(BF16)
        logits = jnp.dot(x, wr_ref[...], preferred_element_type=F32,
                         precision=lax.Precision.HIGHEST)
        idx = lax.broadcasted_iota(jnp.int32, logits.shape, 1)
        v1 = jnp.max(logits, axis=-1, keepdims=True)
        i1 = jnp.min(jnp.where(logits == v1, idx, n_exp), axis=-1, keepdims=True)
        rest = jnp.where(idx == i1, -jnp.inf, logits)
        v2 = jnp.max(rest, axis=-1, keepdims=True)
        i2 = jnp.min(jnp.where(rest == v2, idx, n_exp), axis=-1, keepdims=True)
        t = jnp.exp(v2 - v1)
        g1 = 1.0 / (1.0 + t)
        g2 = t / (1.0 + t)
        comb_ref[...] = jnp.where(idx == i1, g1, 0.0) + jnp.where(idx == i2, g2, 0.0)

    xb = xb_ref[...]
    comb = comb_ref[...]
    idx = lax.broadcasted_iota(jnp.int32, comb.shape, 1)
    w = jnp.sum(jnp.where(idx == e, comb, 0.0), axis=-1, keepdims=True)
    h = _silu(_dot(xb, wg_ref[0])) * _dot(xb, wu_ref[0])
    part = _dot((h * w).astype(BF16), wd_ref[0])

    @pl.when(e == 0)
    def _():
        o_ref[...] = part

    @pl.when(e > 0)
    def _():
        o_ref[...] += part

    @pl.when(e == pl.num_programs(1) - 1)
    def _():
        o_ref[...] = _ln(alpha * x_ref[...] + o_ref[...], pg_ref[...], pb_ref[...])


def _moe(x, wr, wg, wu, wd, pg, pb, alpha, tm):
    m, d = x.shape
    n_exp, _, dexp = wg.shape
    vec = pl.BlockSpec((1, d), lambda i, e: (0, 0))
    return pl.pallas_call(
        functools.partial(_moe_kernel, alpha=alpha),
        grid=(m // tm, n_exp),
        in_specs=[
            pl.BlockSpec((tm, d), lambda i, e: (i, 0)),
            pl.BlockSpec((d, n_exp), lambda i, e: (0, 0)),
            pl.BlockSpec((1, d, dexp), lambda i, e: (e, 0, 0)),
            pl.BlockSpec((1, d, dexp), lambda i, e: (e, 0, 0)),
            pl.BlockSpec((1, dexp, d), lambda i, e: (e, 0, 0)),
            vec, vec,
        ],
        out_specs=pl.BlockSpec((tm, d), lambda i, e: (i, 0)),
        out_shape=jax.ShapeDtypeStruct((m, d), F32),
        scratch_shapes=[pltpu.VMEM((tm, d), BF16), pltpu.VMEM((tm, n_exp), F32)],
        compiler_params=_cparams("parallel", "arbitrary"),
        name="moe",
    )(x, wr, wg, wu, wd, pg, pb)


def kernel(x_prompt, x_sample, state_conv, cache_k, cache_v, page_table, p_prompt, p_sample,
           conv_w_pw1, conv_b_pw1, conv_w_dw, conv_b_dw, conv_ln_g, conv_ln_b, conv_w_pw2,
           kv_w_k, kv_w_v, attn_w_q, attn_lambda, attn_sub_g, attn_w_o,
           ffn_w_gate, ffn_w_up, ffn_w_down, moe_w_router, moe_w_gate, moe_w_up, moe_w_down,
           post_ln_g, post_ln_b, ple_w_proj, ple_w_gate):
    bp, sp, d = x_prompt.shape
    bd, sd, _ = x_sample.shape
    depth = post_ln_g.shape[0]
    n_a = conv_w_pw1.shape[0]
    assert bp == 1 and sd == 1 and depth == 2 and n_a == 1
    n_heads = d // V_DIM
    n_maps = 2 * n_heads
    past = page_table.shape[1] * cache_k.shape[1]
    alpha = (2.0 * depth) ** 0.25
    lam_init = 0.8 - 0.6 * math.exp(-0.3 * 1)

    bf = lambda w: w.astype(BF16)
    row = lambda v: v.reshape(1, -1)

    xp = x_prompt.reshape(sp, d)
    xd = x_sample.reshape(bd, d)
    pp = p_prompt.reshape(depth, sp, -1)
    pdm = p_sample.reshape(depth, bd, -1)

    w1, w2 = bf(conv_w_pw1[0]), bf(conv_w_pw2[0])
    b1 = row(conv_b_pw1[0])
    conv_args = (conv_w_dw[0], row(conv_b_dw[0]), row(conv_ln_g[0]), row(conv_ln_b[0]), w2,
                 row(post_ln_g[0, 0]), row(post_ln_b[0, 0]), alpha)
    up = _glu(xp, w1, b1, 512, 512)
    ud = _glu(xd, w1, b1, bd, 512)
    conv_prompt = up[sp - (CONV_W - 1):].reshape(1, 1, CONV_W - 1, d)
    hist = jnp.transpose(state_conv[0], (1, 0, 2))
    conv_sample = jnp.transpose(jnp.concatenate([hist[1:], ud[None]], axis=0), (1, 0, 2))[None]
    xp = _conv_prompt(up, xp, *conv_args, 256)
    xd = _conv_sample(hist, ud, xd, *conv_args)

    fw = (bf(ffn_w_gate[0]), bf(ffn_w_up[0]), bf(ffn_w_down[0]), row(post_ln_g[0, 1]), row(post_ln_b[0, 1]), alpha)
    xp = _swiglu(xp, *fw, 512, 512)
    xd = _swiglu(xd, *fw, bd, 512)

    pw = (bf(ple_w_gate[0]), bf(ple_w_proj[0]))
    xp = _ple(xp, pp[0], *pw, 512, 512)
    xd = _ple(xd, pdm[0], *pw, bd, 512)

    wk, wv, wq = bf(kv_w_k), bf(kv_w_v), bf(attn_w_q[0])
    cos_p, sin_p = _rope_tables(jnp.arange(sp))
    cos_d, sin_d = _rope_tables(jnp.full((bd,), past))
    k_p, v_p, kb_p, qt_p, vt_p = _proj(xp, wk, wv, wq, cos_p, sin_p, 512, 512, True)
    k_d, v_d, qb_d = _proj(xd, wk, wv, wq, cos_d, sin_d, bd, 512, False)

    lam_p = attn_lambda[0]
    sub_g = row(attn_sub_g[0])
    ap = _attn_prompt(qt_p, kb_p, vt_p, lam_p, attn_sub_g[0].reshape(V_DIM, 1), lam_init, 256, 1024)
    cache_kt = jnp.transpose(cache_k, (0, 2, 3, 1))
    q4 = qb_d.astype(F32).reshape(bd, n_maps, HEAD_DIM, 1)
    kn4 = k_d.astype(BF16).astype(F32).reshape(bd, n_maps, HEAD_DIM, 1)
    ad = _attn_sample(page_table, q4, kn4, v_d.reshape(bd, n_heads, V_DIM), lam_p, sub_g,
                      cache_kt, cache_v, lam_init, 4).reshape(bd, d)
    wo_args = (bf(attn_w_o[0]), row(post_ln_g[1, 0]), row(post_ln_b[1, 0]), alpha)
    xp = _wo(ap, xp, *wo_args, 256)
    xd = _wo(ad, xd, *wo_args, bd)

    mw = (moe_w_router[0], bf(moe_w_gate[0]), bf(moe_w_up[0]), bf(moe_w_down[0]),
          row(post_ln_g[1, 1]), row(post_ln_b[1, 1]), alpha)
    xp = _moe(xp, *mw, 256)
    xd = _moe(xd, *mw, bd)

    pw = (bf(ple_w_gate[1]), bf(ple_w_proj[1]))
    xp = _ple(xp, pp[1], *pw, 512, 512)
    xd = _ple(xd, pdm[1], *pw, bd, 512)

    return (xp.reshape(bp, sp, d), xd.reshape(bd, sd, d), conv_prompt, conv_sample,
            k_p.reshape(bp, sp, n_maps, HEAD_DIM), v_p.reshape(bp, sp, n_heads, V_DIM),
            k_d.reshape(bd, sd, n_maps, HEAD_DIM), v_d.reshape(bd, sd, n_heads, V_DIM))
```

```python
import functools
import math

import jax
import jax.numpy as jnp
from jax import lax
from jax.experimental import pallas as pl
from jax.experimental.pallas import tpu as pltpu

HEAD_DIM = 64
V_DIM = 2 * HEAD_DIM
CONV_W = 31
TOP_K = 2
ROPE_THETA = 10000.0
LN_EPS = 1e-5
ATTN_SCALE = HEAD_DIM ** -0.5
LOG2E = math.log2(math.e)

LANES = 128
SUBLANES = 8
HALO = 32
VMEM_LIMIT = 56 * 1024 * 1024

F32 = jnp.float32
BF16 = jnp.bfloat16


def _cparams(*sem):
    return pltpu.CompilerParams(dimension_semantics=sem, vmem_limit_bytes=VMEM_LIMIT)


def _dot(a, b):
    return jnp.dot(a, b, preferred_element_type=F32)


def _ln(x, g, b):
    mu = jnp.mean(x, axis=-1, keepdims=True)
    xc = x - mu
    var = jnp.mean(xc * xc, axis=-1, keepdims=True)
    return xc * lax.rsqrt(var + LN_EPS) * g + b


def _sigmoid(x):
    return 1.0 / (1.0 + jnp.exp(-x))


def _silu(x):
    return x * _sigmoid(x)


def _glu_kernel(x_ref, wa_ref, wg_ref, ba_ref, bg_ref, u_ref, xb_ref):
    @pl.when(pl.program_id(1) == 0)
    def _():
        xb_ref[...] = x_ref[...].astype(BF16)

    xb = xb_ref[...]
    a = _dot(xb, wa_ref[...]) + ba_ref[...]
    g = _dot(xb, wg_ref[...]) + bg_ref[...]
    u_ref[...] = a * _sigmoid(g)


def _glu(x, w1, b1, tm, tn):
    m, d = x.shape
    nj = d // tn
    return pl.pallas_call(
        _glu_kernel,
        grid=(m // tm, nj),
        in_specs=[
            pl.BlockSpec((tm, d), lambda i, j: (i, 0)),
            pl.BlockSpec((d, tn), lambda i, j: (0, j)),
            pl.BlockSpec((d, tn), lambda i, j: (0, j + nj)),
            pl.BlockSpec((1, tn), lambda i, j: (0, j)),
            pl.BlockSpec((1, tn), lambda i, j: (0, j + nj)),
        ],
        out_specs=pl.BlockSpec((tm, tn), lambda i, j: (i, j)),
        out_shape=jax.ShapeDtypeStruct((m, d), F32),
        scratch_shapes=[pltpu.VMEM((tm, d), BF16)],
        compiler_params=_cparams("parallel", "arbitrary"),
        name="glu",
    )(x, w1, w1, b1, b1)


def _mix_tail(y, x, g_ref, b_ref, w2_ref, pg_ref, pb_ref, alpha):
    y = _silu(_ln(y, g_ref[...], b_ref[...]))
    m = _dot(y.astype(BF16), w2_ref[...])
    return _ln(alpha * x + m, pg_ref[...], pb_ref[...])


def _conv_prompt_kernel(uprev_ref, u_ref, x_ref, wdw_ref, bdw_ref, g_ref, b_ref, w2_ref,
                        pg_ref, pb_ref, o_ref, full_ref, y_ref, *, alpha, rows):
    tm, d = u_ref.shape
    first = pl.program_id(0) == 0
    full_ref[0:HALO, :] = jnp.where(first, 0.0, uprev_ref[...])
    full_ref[HALO:, :] = u_ref[...]
    off = HALO - (CONV_W - 1)

    def col_body(c, carry):
        cs = pl.ds(pl.multiple_of(c * LANES, LANES), LANES)
        w = wdw_ref[:, cs]
        bias = bdw_ref[:, cs]

        def row_body(r, carry2):
            base = pl.multiple_of(r * rows, rows)
            win = full_ref[pl.ds(base, rows + HALO), cs]
            acc = jnp.zeros((rows, LANES), F32)
            for res in range(SUBLANES):
                sh = win if res == 0 else pltpu.roll(win, rows + HALO - res, 0)
                for t in range(CONV_W):
                    if (off + t) % SUBLANES == res:
                        a8 = off + t - res
                        acc = acc + sh[a8:a8 + rows, :] * w[t:t + 1, :]
            y_ref[pl.ds(base, rows), cs] = acc + bias
            return carry2

        return lax.fori_loop(0, tm // rows, row_body, carry)

    lax.fori_loop(0, d // LANES, col_body, 0)
    o_ref[...] = _mix_tail(y_ref[...], x_ref[...], g_ref, b_ref, w2_ref, pg_ref, pb_ref, alpha)


def _conv_prompt(u, x, wdw, bdw, g, b, w2, pg, pb, alpha, tm):
    m, d = x.shape
    nh = tm // HALO
    vec = pl.BlockSpec((1, d), lambda i: (0, 0))
    return pl.pallas_call(
        functools.partial(_conv_prompt_kernel, alpha=alpha, rows=32),
        grid=(m // tm,),
        in_specs=[
            pl.BlockSpec((HALO, d), lambda i: (jnp.maximum(i * nh - 1, 0), 0)),
            pl.BlockSpec((tm, d), lambda i: (i, 0)),
            pl.BlockSpec((tm, d), lambda i: (i, 0)),
            pl.BlockSpec((CONV_W, d), lambda i: (0, 0)),
            vec, vec, vec,
            pl.BlockSpec((d, d), lambda i: (0, 0)),
            vec, vec,
        ],
        out_specs=pl.BlockSpec((tm, d), lambda i: (i, 0)),
        out_shape=jax.ShapeDtypeStruct((m, d), F32),
        scratch_shapes=[pltpu.VMEM((tm + HALO, d), F32), pltpu.VMEM((tm, d), F32)],
        compiler_params=_cparams("parallel"),
        name="conv_prompt",
    )(u, u, x, wdw, bdw, g, b, w2, pg, pb)


def _conv_sample_kernel(hist_ref, u_ref, x_ref, wdw_ref, bdw_ref, g_ref, b_ref, w2_ref,
                        pg_ref, pb_ref, o_ref, *, alpha):
    y = u_ref[...] * wdw_ref[CONV_W - 1:CONV_W, :]
    for t in range(CONV_W - 1):
        y = y + hist_ref[t] * wdw_ref[t:t + 1, :]
    y = y + bdw_ref[...]
    o_ref[...] = _mix_tail(y, x_ref[...], g_ref, b_ref, w2_ref, pg_ref, pb_ref, alpha)


def _conv_sample(hist, u, x, wdw, bdw, g, b, w2, pg, pb, alpha):
    m, d = x.shape
    return pl.pallas_call(
        functools.partial(_conv_sample_kernel, alpha=alpha),
        out_shape=jax.ShapeDtypeStruct((m, d), F32),
        compiler_params=pltpu.CompilerParams(vmem_limit_bytes=VMEM_LIMIT),
        name="conv_sample",
    )(hist, u, x, wdw, bdw, g, b, w2, pg, pb)


def _swiglu_kernel(x_ref, wg_ref, wu_ref, wd_ref, pg_ref, pb_ref, o_ref, xb_ref, *, alpha):
    f = pl.program_id(1)

    @pl.when(f == 0)
    def _():
        xb_ref[...] = x_ref[...].astype(BF16)

    xb = xb_ref[...]
    h = _silu(_dot(xb, wg_ref[...])) * _dot(xb, wu_ref[...])
    part = _dot(h.astype(BF16), wd_ref[...])

    @pl.when(f == 0)
    def _():
        o_ref[...] = part

    @pl.when(f > 0)
    def _():
        o_ref[...] += part

    @pl.when(f == pl.num_programs(1) - 1)
    def _():
        o_ref[...] = _ln(alpha * x_ref[...] + o_ref[...], pg_ref[...], pb_ref[...])


def _swiglu(x, wg, wu, wd, pg, pb, alpha, tm, tf):
    m, d = x.shape
    dff = wg.shape[1]
    vec = pl.BlockSpec((1, d), lambda i, f: (0, 0))
    return pl.pallas_call(
        functools.partial(_swiglu_kernel, alpha=alpha),
        grid=(m // tm, dff // tf),
        in_specs=[
            pl.BlockSpec((tm, d), lambda i, f: (i, 0)),
            pl.BlockSpec((d, tf), lambda i, f: (0, f)),
            pl.BlockSpec((d, tf), lambda i, f: (0, f)),
            pl.BlockSpec((tf, d), lambda i, f: (f, 0)),
            vec, vec,
        ],
        out_specs=pl.BlockSpec((tm, d), lambda i, f: (i, 0)),
        out_shape=jax.ShapeDtypeStruct((m, d), F32),
        scratch_shapes=[pltpu.VMEM((tm, d), BF16)],
        compiler_params=_cparams("parallel", "arbitrary"),
        name="swiglu",
    )(x, wg, wu, wd, pg, pb)


def _ple_kernel(x_ref, xc_ref, p_ref, wg_ref, wp_ref, o_ref, xb_ref):
    @pl.when(pl.program_id(1) == 0)
    def _():
        xb_ref[...] = x_ref[...].astype(BF16)

    gate = _sigmoid(_dot(xb_ref[...], wg_ref[...]))
    o_ref[...] = xc_ref[...] + gate * _dot(p_ref[...].astype(BF16), wp_ref[...])


def _ple(x, p, wg, wp, tm, tn):
    m, d = x.shape
    pd = p.shape[1]
    return pl.pallas_call(
        _ple_kernel,
        grid=(m // tm, d // tn),
        in_specs=[
            pl.BlockSpec((tm, d), lambda i, j: (i, 0)),
            pl.BlockSpec((tm, tn), lambda i, j: (i, j)),
            pl.BlockSpec((tm, pd), lambda i, j: (i, 0)),
            pl.BlockSpec((d, tn), lambda i, j: (0, j)),
            pl.BlockSpec((pd, tn), lambda i, j: (0, j)),
        ],
        out_specs=pl.BlockSpec((tm, tn), lambda i, j: (i, j)),
        out_shape=jax.ShapeDtypeStruct((m, d), F32),
        scratch_shapes=[pltpu.VMEM((tm, d), BF16)],
        compiler_params=_cparams("parallel", "arbitrary"),
        name="ple",
    )(x, x, p, wg, wp)


def _rope_tables(pos):
    half = HEAD_DIM // 2
    inv = ROPE_THETA ** (-jnp.arange(half, dtype=F32) / half)
    ang = pos.astype(F32)[:, None] * inv[None, :]
    cos, sin = jnp.cos(ang), jnp.sin(ang)
    cos_t = jnp.concatenate([cos, cos, cos, cos], axis=-1)
    sin_t = jnp.concatenate([-sin, sin, -sin, sin], axis=-1)
    return cos_t, sin_t


def _rope_tile(h, cos, sin, first_half):
    partner = jnp.where(first_half, pltpu.roll(h, LANES - HEAD_DIM // 2, 1),
                        pltpu.roll(h, HEAD_DIM // 2, 1))
    return h * cos + partner * sin


def _proj_kernel(x_ref, wk_ref, wv_ref, wq_ref, cos_ref, sin_ref, *refs, transposed):
    if transposed:
        k_ref, v_ref, kb_ref, qt_ref, vt_ref, xb_ref = refs
    else:
        k_ref, v_ref, qb_ref, xb_ref = refs

    @pl.when(pl.program_id(1) == 0)
    def _():
        xb_ref[...] = x_ref[...].astype(BF16)

    xb = xb_ref[...]
    tm, tn = k_ref.shape
    cos, sin = cos_ref[...], sin_ref[...]
    lane = lax.broadcasted_iota(jnp.int32, (tm, LANES), 1)
    first_half = (lane % HEAD_DIM) < (HEAD_DIM // 2)
    hk = _dot(xb, wk_ref[...])
    hq = _dot(xb, wq_ref[...])
    for c in range(tn // LANES):
        cs = slice(c * LANES, (c + 1) * LANES)
        kr = _rope_tile(hk[:, cs], cos, sin, first_half)
        k_ref[:, cs] = kr
        qr = _rope_tile(hq[:, cs], cos, sin, first_half)
        if transposed:
            kb_ref[:, cs] = kr.astype(BF16)
            qt_ref[cs, :] = (qr * (ATTN_SCALE * LOG2E)).T.astype(BF16)
        else:
            qb_ref[:, cs] = (qr * ATTN_SCALE).astype(BF16)
    hv = _dot(xb, wv_ref[...])
    v_ref[...] = hv
    if transposed:
        vt_ref[...] = hv.T.astype(BF16)


def _proj(x, wk, wv, wq, cos_t, sin_t, tm, tn, transposed):
    m, d = x.shape
    wspec = pl.BlockSpec((d, tn), lambda i, j: (0, j))
    tspec = pl.BlockSpec((tm, LANES), lambda i, j: (i, 0))
    ospec = pl.BlockSpec((tm, tn), lambda i, j: (i, j))
    tr_spec = pl.BlockSpec((tn, tm), lambda i, j: (j, i))
    f32_out = jax.ShapeDtypeStruct((m, d), F32)
    if transposed:
        out_specs = [ospec, ospec, ospec, tr_spec, tr_spec]
        out_shape = [f32_out, f32_out, jax.ShapeDtypeStruct((m, d), BF16),
                     jax.ShapeDtypeStruct((d, m), BF16), jax.ShapeDtypeStruct((d, m), BF16)]
    else:
        out_specs = [ospec, ospec, ospec]
        out_shape = [f32_out, f32_out, jax.ShapeDtypeStruct((m, d), BF16)]
    return pl.pallas_call(
        functools.partial(_proj_kernel, transposed=transposed),
        grid=(m // tm, d // tn),
        in_specs=[pl.BlockSpec((tm, d), lambda i, j: (i, 0)), wspec, wspec, wspec, tspec, tspec],
        out_specs=out_specs,
        out_shape=out_shape,
        scratch_shapes=[pltpu.VMEM((tm, d), BF16)],
        compiler_params=_cparams("parallel", "arbitrary"),
        name="proj",
    )(x, wk, wv, wq, cos_t, sin_t)


def _diff_lambda(lam_ref, lam_init):
    lp = lam_ref[...]
    s1 = jnp.sum(lp[0:1, :] * lp[1:2, :], axis=-1, keepdims=True)
    s2 = jnp.sum(lp[2:3, :] * lp[3:4, :], axis=-1, keepdims=True)
    return jnp.exp(s1) - jnp.exp(s2) + lam_init


def _diff_finalize(o1, o2, lam, g, lam_init):
    o = o1 - lam * o2
    o = o * lax.rsqrt(jnp.mean(o * o, axis=-1, keepdims=True) + LN_EPS)
    return o * g * (1.0 - lam_init)


def _attn_prompt_kernel(qt_ref, k_ref, vt_ref, lam_ref, g_ref, o_ref,
                        sa_ref, sb_ref, m_ref, l_ref, acc_ref, *, lam_init, tk):
    tq = qt_ref.shape[1]
    i = pl.program_id(1)
    q0 = i * tq
    n_full = q0 // tk
    qt = qt_ref[...]
    feat = lax.broadcasted_iota(jnp.int32, qt.shape, 0)
    qz = jnp.concatenate([jnp.where(feat < HEAD_DIM, qt, 0), jnp.where(feat >= HEAD_DIM, qt, 0)], axis=1)
    m_ref[...] = jnp.full(m_ref.shape, -jnp.inf, F32)
    l_ref[...] = jnp.zeros(l_ref.shape, F32)
    acc_ref[...] = jnp.zeros(acc_ref.shape, F32)

    def scores(b, dst_ref):
        base = pl.multiple_of(b * tk, tk)
        dst_ref[...] = _dot(k_ref[pl.ds(base, tk), :], qz)

    def softmax_pv(b, src_ref, masked):
        base = pl.multiple_of(b * tk, tk)
        s = src_ref[...]
        if masked:
            key = base + lax.broadcasted_iota(jnp.int32, s.shape, 0)
            qry = q0 + lax.broadcasted_iota(jnp.int32, s.shape, 1) % tq
            s = jnp.where(key <= qry, s, -jnp.inf)
        m_old = m_ref[...]
        m_new = jnp.maximum(m_old, jnp.max(s, axis=0, keepdims=True))
        a = jnp.exp2(m_old - m_new)
        p = jnp.exp2(s - m_new)
        l_ref[...] = a * l_ref[...] + jnp.sum(p, axis=0, keepdims=True)
        m_ref[...] = m_new
        acc_ref[...] = a * acc_ref[...] + _dot(vt_ref[:, pl.ds(base, tk)], p.astype(BF16))

    scores(0, sa_ref)

    def pair(t, carry):
        b = 2 * t
        scores(b + 1, sb_ref)
        softmax_pv(b, sa_ref, False)
        scores(b + 2, sa_ref)
        softmax_pv(b + 1, sb_ref, False)
        return carry

    lax.fori_loop(0, n_full // 2, pair, 0)

    @pl.when(n_full % 2 == 1)
    def _():
        scores(n_full, sb_ref)
        softmax_pv(n_full - 1, sa_ref, False)
        softmax_pv(n_full, sb_ref, True)

    @pl.when(n_full % 2 == 0)
    def _():
        softmax_pv(n_full, sa_ref, True)

    lam = _diff_lambda(lam_ref, lam_init)
    o = acc_ref[...] * (1.0 / l_ref[...])
    ot = o[:, :tq] - lam * o[:, tq:]
    ot = ot * lax.rsqrt(jnp.mean(ot * ot, axis=0, keepdims=True) + LN_EPS)
    ot = ot * g_ref[...] * (1.0 - lam_init)
    o_ref[...] = ot.T.astype(o_ref.dtype)


def _attn_prompt(qt, kb, vt, lam_p, sub_g_col, lam_init, tq, tk):
    d, s = qt.shape
    nh = d // V_DIM
    return pl.pallas_call(
        functools.partial(_attn_prompt_kernel, lam_init=lam_init, tk=tk),
        grid=(nh, s // tq),
        in_specs=[
            pl.BlockSpec((V_DIM, tq), lambda h, i: (h, i)),
            pl.BlockSpec((s, V_DIM), lambda h, i: (0, h)),
            pl.BlockSpec((V_DIM, s), lambda h, i: (h, 0)),
            pl.BlockSpec(lam_p.shape, lambda h, i: (0, 0)),
            pl.BlockSpec((V_DIM, 1), lambda h, i: (0, 0)),
        ],
        out_specs=pl.BlockSpec((tq, V_DIM), lambda h, i: (i, h)),
        out_shape=jax.ShapeDtypeStruct((s, d), BF16),
        scratch_shapes=[pltpu.VMEM((tk, 2 * tq), F32), pltpu.VMEM((tk, 2 * tq), F32),
                        pltpu.VMEM((1, 2 * tq), F32), pltpu.VMEM((1, 2 * tq), F32),
                        pltpu.VMEM((V_DIM, 2 * tq), F32)],
        compiler_params=_cparams("parallel", "arbitrary"),
        name="attn_prompt",
    )(qt, kb, vt, lam_p, sub_g_col)


def _attn_sample_kernel(pt_ref, q_ref, kn_ref, vn_ref, lam_ref, g_ref, e_ref, *refs,
                        lam_init, pages_per_step):
    k_refs = refs[:pages_per_step]
    v_refs = refs[pages_per_step:2 * pages_per_step]
    o_ref = refs[2 * pages_per_step]
    qb_ref, m_ref, l_ref, acc_ref = refs[2 * pages_per_step + 1:]
    step_idx = pl.program_id(1)
    n_maps, _, page = qb_ref.shape
    n_heads = n_maps // 2

    @pl.when(step_idx == 0)
    def _():
        q = q_ref[0]
        qb_ref[...] = jnp.broadcast_to(q, qb_ref.shape)
        s_self = jnp.sum(q * kn_ref[0], axis=1)
        m_ref[...] = s_self
        l_ref[...] = jnp.ones_like(l_ref)
        vn = vn_ref[0]
        for h in range(n_heads):
            acc_ref[2 * h:2 * h + 2, :] = jnp.broadcast_to(vn[h:h + 1, :], (2, V_DIM))

    slot = lax.broadcasted_iota(jnp.int32, (n_maps, page * n_heads), 1)
    mp_row = lax.broadcasted_iota(jnp.int32, (n_maps, page * n_heads), 0)
    own_head = (slot % n_heads) == (mp_row >> 1)
    for kp_ref, vp_ref in zip(k_refs, v_refs):
        s = jnp.sum(kp_ref[0] * qb_ref[...], axis=1)
        m_old = m_ref[...]
        m_new = jnp.maximum(m_old, jnp.max(s, axis=-1, keepdims=True))
        a = jnp.exp(m_old - m_new)
        p = jnp.exp(s - m_new)
        l_ref[...] = a * l_ref[...] + jnp.sum(p, axis=-1, keepdims=True)
        m_ref[...] = m_new
        pe = _dot(p.astype(BF16), e_ref[...])
        pm = jnp.where(own_head, pe, 0.0).astype(BF16)
        vb = vp_ref[0].reshape(page * n_heads, V_DIM).astype(BF16)
        acc_ref[...] = a * acc_ref[...] + _dot(pm, vb)

    @pl.when(step_idx == pl.num_programs(1) - 1)
    def _():
        acc_ref[...] = acc_ref[...] / l_ref[...]
        o1 = acc_ref[pl.ds(0, n_heads, stride=2), :]
        o2 = acc_ref[pl.ds(1, n_heads, stride=2), :]
        lam = _diff_lambda(lam_ref, lam_init)
        o_ref[0] = _diff_finalize(o1, o2, lam, g_ref[...], lam_init).astype(o_ref.dtype)


def _attn_sample(page_table, q4, kn4, vn3, lam_p, sub_g, cache_kt, cache_v, lam_init, pages_per_step):
    nb, n_pages = page_table.shape
    _, n_maps, hd, page = cache_kt.shape
    n_heads = cache_v.shape[2]
    pps = pages_per_step

    def kspec(j):
        return pl.BlockSpec((1, n_maps, hd, page), lambda b, s, pt: (pt[b, s * pps + j], 0, 0, 0))

    def vspec(j):
        return pl.BlockSpec((1, page, n_heads, V_DIM), lambda b, s, pt: (pt[b, s * pps + j], 0, 0, 0))

    expand = (jnp.arange(page * n_heads)[None, :] // n_heads == jnp.arange(page)[:, None]).astype(BF16)
    grid_spec = pltpu.PrefetchScalarGridSpec(
        num_scalar_prefetch=1,
        grid=(nb, n_pages // pps),
        in_specs=[
            pl.BlockSpec((1, n_maps, hd, 1), lambda b, s, pt: (b, 0, 0, 0)),
            pl.BlockSpec((1, n_maps, hd, 1), lambda b, s, pt: (b, 0, 0, 0)),
            pl.BlockSpec((1, n_heads, V_DIM), lambda b, s, pt: (b, 0, 0)),
            pl.BlockSpec(lam_p.shape, lambda b, s, pt: (0, 0)),
            pl.BlockSpec((1, V_DIM), lambda b, s, pt: (0, 0)),
            pl.BlockSpec(expand.shape, lambda b, s, pt: (0, 0)),
        ] + [kspec(j) for j in range(pps)] + [vspec(j) for j in range(pps)],
        out_specs=pl.BlockSpec((1, n_heads, V_DIM), lambda b, s, pt: (b, 0, 0)),
        scratch_shapes=[
            pltpu.VMEM((n_maps, hd, page), F32),
            pltpu.VMEM((n_maps, 1), F32),
            pltpu.VMEM((n_maps, 1), F32),
            pltpu.VMEM((n_maps, V_DIM), F32),
        ],
    )
    return pl.pallas_call(
        functools.partial(_attn_sample_kernel, lam_init=lam_init, pages_per_step=pps),
        grid_spec=grid_spec,
        out_shape=jax.ShapeDtypeStruct((nb, n_heads, V_DIM), BF16),
        compiler_params=_cparams("parallel", "arbitrary"),
        name="attn_sample",
    )(page_table, q4, kn4, vn3, lam_p, sub_g, expand, *([cache_kt] * pps), *([cache_v] * pps))


def _wo_kernel(a_ref, x_ref, wo_ref, pg_ref, pb_ref, o_ref, *, alpha):
    m = _dot(a_ref[...], wo_ref[...])
    o_ref[...] = _ln(alpha * x_ref[...] + m, pg_ref[...], pb_ref[...])


def _wo(a, x, wo, pg, pb, alpha, tm):
    m, d = x.shape
    vec = pl.BlockSpec((1, d), lambda i: (0, 0))
    return pl.pallas_call(
        functools.partial(_wo_kernel, alpha=alpha),
        grid=(m // tm,),
        in_specs=[
            pl.BlockSpec((tm, d), lambda i: (i, 0)),
            pl.BlockSpec((tm, d), lambda i: (i, 0)),
            pl.BlockSpec((d, d), lambda i: (0, 0)),
            vec, vec,
        ],
        out_specs=pl.BlockSpec((tm, d), lambda i: (i, 0)),
        out_shape=jax.ShapeDtypeStruct((m, d), F32),
        compiler_params=_cparams("parallel"),
        name="wo",
    )(a, x, wo, pg, pb)


def _moe_kernel(x_ref, wr_ref, wg_ref, wu_ref, wd_ref, pg_ref, pb_ref, o_ref, xb_ref, comb_ref,
                *, alpha):
    e = pl.program_id(1)
    n_exp = wr_ref.shape[1]

    @pl.when(e == 0)
    def _():
        x = x_ref[...]
        xb_ref[...] = x.astype(BF16)
        logits = jnp.dot(x, wr_ref[...], preferred_element_type=F32,
                         precision=lax.Precision.HIGHEST)
        idx = lax.broadcasted_iota(jnp.int32, logits.shape, 1)
        v1 = jnp.max(logits, axis=-1, keepdims=True)
        i1 = jnp.min(jnp.where(logits == v1, idx, n_exp), axis=-1, keepdims=True)
        rest = jnp.where(idx == i1, -jnp.inf, logits)
        v2 = jnp.max(rest, axis=-1, keepdims=True)
        i2 = jnp.min(jnp.where(rest == v2, idx, n_exp), axis=-1, keepdims=True)
        t = jnp.exp(v2 - v1)
        g1 = 1.0 / (1.0 + t)
        g2 = t / (1.0 + t)
        comb_ref[...] = jnp.where(idx == i1, g1, 0.0) + jnp.where(idx == i2, g2, 0.0)

    xb = xb_ref[...]
    comb = comb_ref[...]
    idx = lax.broadcasted_iota(jnp.int32, comb.shape, 1)
    w = jnp.sum(jnp.where(idx == e, comb, 0.0), axis=-1, keepdims=True)
    h = _silu(_dot(xb, wg_ref[0])) * _dot(xb, wu_ref[0])
    part = _dot((h * w).astype(BF16), wd_ref[0])

    @pl.when(e == 0)
    def _():
        o_ref[...] = part

    @pl.when(e > 0)
    def _():
        o_ref[...] += part

    @pl.when(e == pl.num_programs(1) - 1)
    def _():
        o_ref[...] = _ln(alpha * x_ref[...] + o_ref[...], pg_ref[...], pb_ref[...])


def _moe(x, wr, wg, wu, wd, pg, pb, alpha, tm):
    m, d = x.shape
    n_exp, _, dexp = wg.shape
    vec = pl.BlockSpec((1, d), lambda i, e: (0, 0))
    return pl.pallas_call(
        functools.partial(_moe_kernel, alpha=alpha),
        grid=(m // tm, n_exp),
        in_specs=[
            pl.BlockSpec((tm, d), lambda i, e: (i, 0)),
            pl.BlockSpec((d, n_exp), lambda i, e: (0, 0)),
            pl.BlockSpec((1, d, dexp), lambda i, e: (e, 0, 0)),
            pl.BlockSpec((1, d, dexp), lambda i, e: (e, 0, 0)),
            pl.BlockSpec((1, dexp, d), lambda i, e: (e, 0, 0)),
            vec, vec,
        ],
        out_specs=pl.BlockSpec((tm, d), lambda i, e: (i, 0)),
        out_shape=jax.ShapeDtypeStruct((m, d), F32),
        scratch_shapes=[pltpu.VMEM((tm, d), BF16), pltpu.VMEM((tm, n_exp), F32)],
        compiler_params=_cparams("parallel", "arbitrary"),
        name="moe",
    )(x, wr, wg, wu, wd, pg, pb)


def kernel(x_prompt, x_sample, state_conv, cache_k, cache_v, page_table, p_prompt, p_sample,
           conv_w_pw1, conv_b_pw1, conv_w_dw, conv_b_dw, conv_ln_g, conv_ln_b, conv_w_pw2,
           kv_w_k, kv_w_v, attn_w_q, attn_lambda, attn_sub_g, attn_w_o,
           ffn_w_gate, ffn_w_up, ffn_w_down, moe_w_router, moe_w_gate, moe_w_up, moe_w_down,
           post_ln_g, post_ln_b, ple_w_proj, ple_w_gate):
    bp, sp, d = x_prompt.shape
    bd, sd, _ = x_sample.shape
    depth = post_ln_g.shape[0]
    n_a = conv_w_pw1.shape[0]
    assert bp == 1 and sd == 1 and depth == 2 and n_a == 1
    n_heads = d // V_DIM
    n_maps = 2 * n_heads
    past = page_table.shape[1] * cache_k.shape[1]
    alpha = (2.0 * depth) ** 0.25
    lam_init = 0.8 - 0.6 * math.exp(-0.3 * 1)

    bf = lambda w: w.astype(BF16)
    row = lambda v: v.reshape(1, -1)

    xp = x_prompt.reshape(sp, d)
    xd = x_sample.reshape(bd, d)
    pp = p_prompt.reshape(depth, sp, -1)
    pdm = p_sample.reshape(depth, bd, -1)

    w1, w2 = bf(conv_w_pw1[0]), bf(conv_w_pw2[0])
    b1 = row(conv_b_pw1[0])
    conv_args = (conv_w_dw[0], row(conv_b_dw[0]), row(conv_ln_g[0]), row(conv_ln_b[0]), w2,
                 row(post_ln_g[0, 0]), row(post_ln_b[0, 0]), alpha)
    up = _glu(xp, w1, b1, 512, 512)
    ud = _glu(xd, w1, b1, bd, 512)
    conv_prompt = up[sp - (CONV_W - 1):].reshape(1, 1, CONV_W - 1, d)
    hist = jnp.transpose(state_conv[0], (1, 0, 2))
    conv_sample = jnp.transpose(jnp.concatenate([hist[1:], ud[None]], axis=0), (1, 0, 2))[None]
    xp = _conv_prompt(up, xp, *conv_args, 256)
    xd = _conv_sample(hist, ud, xd, *conv_args)

    fw = (bf(ffn_w_gate[0]), bf(ffn_w_up[0]), bf(ffn_w_down[0]), row(post_ln_g[0, 1]), row(post_ln_b[0, 1]), alpha)
    xp = _swiglu(xp, *fw, 512, 512)
    xd = _swiglu(xd, *fw, bd, 512)

    pw = (bf(ple_w_gate[0]), bf(ple_w_proj[0]))
    xp = _ple(xp, pp[0], *pw, 512, 512)
    xd = _ple(xd, pdm[0], *pw, bd, 512)

    wk, wv, wq = bf(kv_w_k), bf(kv_w_v), bf(attn_w_q[0])
    cos_p, sin_p = _rope_tables(jnp.arange(sp))
    cos_d, sin_d = _rope_tables(jnp.full((bd,), past))
    k_p, v_p, kb_p, qt_p, vt_p = _proj(xp, wk, wv, wq, cos_p, sin_p, 512, 512, True)
    k_d, v_d, qb_d = _proj(xd, wk, wv, wq, cos_d, sin_d, bd, 512, False)

    lam_p = attn_lambda[0]
    sub_g = row(attn_sub_g[0])
    ap = _attn_prompt(qt_p, kb_p, vt_p, lam_p, attn_sub_g[0].reshape(V_DIM, 1), lam_init, 256, 1024)
    cache_kt = jnp.transpose(cache_k, (0, 2, 3, 1))
    q4 = qb_d.astype(F32).reshape(bd, n_maps, HEAD_DIM, 1)
    kn4 = k_d.astype(BF16).astype(F32).reshape(bd, n_maps, HEAD_DIM, 1)
    ad = _attn_sample(page_table, q4, kn4, v_d.reshape(bd, n_heads, V_DIM), lam_p, sub_g,
                      cache_kt, cache_v, lam_init, 8).reshape(bd, d)
    wo_args = (bf(attn_w_o[0]), row(post_ln_g[1, 0]), row(post_ln_b[1, 0]), alpha)
    xp = _wo(ap, xp, *wo_args, 256)
    xd = _wo(ad, xd, *wo_args, bd)

    mw = (moe_w_router[0], bf(moe_w_gate[0]), bf(moe_w_up[0]), bf(moe_w_down[0]),
          row(post_ln_g[1, 1]), row(post_ln_b[1, 1]), alpha)
    xp = _moe(xp, *mw, 256)
    xd = _moe(xd, *mw, bd)

    pw = (bf(ple_w_gate[1]), bf(ple_w_proj[1]))
    xp = _ple(xp, pp[1], *pw, 512, 512)
    xd = _ple(xd, pdm[1], *pw, bd, 512)

    return (xp.reshape(bp, sp, d), xd.reshape(bd, sd, d), conv_prompt, conv_sample,
            k_p.reshape(bp, sp, n_maps, HEAD_DIM), v_p.reshape(bp, sp, n_heads, V_DIM),
            k_d.reshape(bd, sd, n_maps, HEAD_DIM), v_d.reshape(bd, sd, n_heads, V_DIM))
```

```python
import functools
import math

import jax
import jax.numpy as jnp
from jax import lax
from jax.experimental import pallas as pl
from jax.experimental.pallas import tpu as pltpu

HEAD_DIM = 64
V_DIM = 2 * HEAD_DIM
CONV_W = 31
TOP_K = 2
ROPE_THETA = 10000.0
LN_EPS = 1e-5
ATTN_SCALE = HEAD_DIM ** -0.5
LOG2E = math.log2(math.e)

LANES = 128
SUBLANES = 8
HALO = 32
VMEM_LIMIT = 56 * 1024 * 1024

F32 = jnp.float32
BF16 = jnp.bfloat16


def _cparams(*sem):
    return pltpu.CompilerParams(dimension_semantics=sem, vmem_limit_bytes=VMEM_LIMIT)


def _dot(a, b):
    return jnp.dot(a, b, preferred_element_type=F32)


def _ln(x, g, b):
    mu = jnp.mean(x, axis=-1, keepdims=True)
    xc = x - mu
    var = jnp.mean(xc * xc, axis=-1, keepdims=True)
    return xc * lax.rsqrt(var + LN_EPS) * g + b


def _sigmoid(x):
    return 1.0 / (1.0 + jnp.exp(-x))


def _silu(x):
    return x * _sigmoid(x)


def _glu_kernel(x_ref, wa_ref, wg_ref, ba_ref, bg_ref, u_ref, xb_ref):
    @pl.when(pl.program_id(1) == 0)
    def _():
        xb_ref[...] = x_ref[...].astype(BF16)

    xb = xb_ref[...]
    a = _dot(xb, wa_ref[...]) + ba_ref[...]
    g = _dot(xb, wg_ref[...]) + bg_ref[...]
    u_ref[...] = a * _sigmoid(g)


def _glu(x, w1, b1, tm, tn):
    m, d = x.shape
    nj = d // tn
    return pl.pallas_call(
        _glu_kernel,
        grid=(m // tm, nj),
        in_specs=[
            pl.BlockSpec((tm, d), lambda i, j: (i, 0)),
            pl.BlockSpec((d, tn), lambda i, j: (0, j)),
            pl.BlockSpec((d, tn), lambda i, j: (0, j + nj)),
            pl.BlockSpec((1, tn), lambda i, j: (0, j)),
            pl.BlockSpec((1, tn), lambda i, j: (0, j + nj)),
        ],
        out_specs=pl.BlockSpec((tm, tn), lambda i, j: (i, j)),
        out_shape=jax.ShapeDtypeStruct((m, d), F32),
        scratch_shapes=[pltpu.VMEM((tm, d), BF16)],
        compiler_params=_cparams("parallel", "arbitrary"),
        name="glu",
    )(x, w1, w1, b1, b1)


def _mix_tail(y, x, g_ref, b_ref, w2_ref, pg_ref, pb_ref, alpha):
    y = _silu(_ln(y, g_ref[...], b_ref[...]))
    m = _dot(y.astype(BF16), w2_ref[...])
    return _ln(alpha * x + m, pg_ref[...], pb_ref[...])


def _conv_prompt_kernel(uprev_ref, u_ref, x_ref, wdw_ref, bdw_ref, g_ref, b_ref, w2_ref,
                        pg_ref, pb_ref, o_ref, full_ref, y_ref, *, alpha, rows):
    tm, d = u_ref.shape
    first = pl.program_id(0) == 0
    full_ref[0:HALO, :] = jnp.where(first, 0.0, uprev_ref[...])
    full_ref[HALO:, :] = u_ref[...]
    off = HALO - (CONV_W - 1)

    def col_body(c, carry):
        cs = pl.ds(pl.multiple_of(c * LANES, LANES), LANES)
        w = wdw_ref[:, cs]
        bias = bdw_ref[:, cs]

        def row_body(r, carry2):
            base = pl.multiple_of(r * rows, rows)
            win = full_ref[pl.ds(base, rows + HALO), cs]
            acc = jnp.zeros((rows, LANES), F32)
            for res in range(SUBLANES):
                sh = win if res == 0 else pltpu.roll(win, rows + HALO - res, 0)
                for t in range(CONV_W):
                    if (off + t) % SUBLANES == res:
                        a8 = off + t - res
                        acc = acc + sh[a8:a8 + rows, :] * w[t:t + 1, :]
            y_ref[pl.ds(base, rows), cs] = acc + bias
            return carry2

        return lax.fori_loop(0, tm // rows, row_body, carry)

    lax.fori_loop(0, d // LANES, col_body, 0)
    o_ref[...] = _mix_tail(y_ref[...], x_ref[...], g_ref, b_ref, w2_ref, pg_ref, pb_ref, alpha)


def _conv_prompt(u, x, wdw, bdw, g, b, w2, pg, pb, alpha, tm):
    m, d = x.shape
    nh = tm // HALO
    vec = pl.BlockSpec((1, d), lambda i: (0, 0))
    return pl.pallas_call(
        functools.partial(_conv_prompt_kernel, alpha=alpha, rows=32),
        grid=(m // tm,),
        in_specs=[
            pl.BlockSpec((HALO, d), lambda i: (jnp.maximum(i * nh - 1, 0), 0)),
            pl.BlockSpec((tm, d), lambda i: (i, 0)),
            pl.BlockSpec((tm, d), lambda i: (i, 0)),
            pl.BlockSpec((CONV_W, d), lambda i: (0, 0)),
            vec, vec, vec,
            pl.BlockSpec((d, d), lambda i: (0, 0)),
            vec, vec,
        ],
        out_specs=pl.BlockSpec((tm, d), lambda i: (i, 0)),
        out_shape=jax.ShapeDtypeStruct((m, d), F32),
        scratch_shapes=[pltpu.VMEM((tm + HALO, d), F32), pltpu.VMEM((tm, d), F32)],
        compiler_params=_cparams("parallel"),
        name="conv_prompt",
    )(u, u, x, wdw, bdw, g, b, w2, pg, pb)


def _conv_sample_kernel(hist_ref, u_ref, x_ref, wdw_ref, bdw_ref, g_ref, b_ref, w2_ref,
                        pg_ref, pb_ref, o_ref, *, alpha):
    y = u_ref[...] * wdw_ref[CONV_W - 1:CONV_W, :]
    for t in range(CONV_W - 1):
        y = y + hist_ref[t] * wdw_ref[t:t + 1, :]
    y = y + bdw_ref[...]
    o_ref[...] = _mix_tail(y, x_ref[...], g_ref, b_ref, w2_ref, pg_ref, pb_ref, alpha)


def _conv_sample(hist, u, x, wdw, bdw, g, b, w2, pg, pb, alpha):
    m, d = x.shape
    return pl.pallas_call(
        functools.partial(_conv_sample_kernel, alpha=alpha),
        out_shape=jax.ShapeDtypeStruct((m, d), F32),
        compiler_params=pltpu.CompilerParams(vmem_limit_bytes=VMEM_LIMIT),
        name="conv_sample",
    )(hist, u, x, wdw, bdw, g, b, w2, pg, pb)


def _swiglu_kernel(x_ref, wg_ref, wu_ref, wd_ref, pg_ref, pb_ref, o_ref, xb_ref, *, alpha):
    f = pl.program_id(1)

    @pl.when(f == 0)
    def _():
        xb_ref[...] = x_ref[...].astype(BF16)

    xb = xb_ref[...]
    h = _silu(_dot(xb, wg_ref[...])) * _dot(xb, wu_ref[...])
    part = _dot(h.astype(BF16), wd_ref[...])

    @pl.when(f == 0)
    def _():
        o_ref[...] = part

    @pl.when(f > 0)
    def _():
        o_ref[...] += part

    @pl.when(f == pl.num_programs(1) - 1)
    def _():
        o_ref[...] = _ln(alpha * x_ref[...] + o_ref[...], pg_ref[...], pb_ref[...])


def _swiglu(x, wg, wu, wd, pg, pb, alpha, tm, tf):
    m, d = x.shape
    dff = wg.shape[1]
    vec = pl.BlockSpec((1, d), lambda i, f: (0, 0))
    return pl.pallas_call(
        functools.partial(_swiglu_kernel, alpha=alpha),
        grid=(m // tm, dff // tf),
        in_specs=[
            pl.BlockSpec((tm, d), lambda i, f: (i, 0)),
            pl.BlockSpec((d, tf), lambda i, f: (0, f)),
            pl.BlockSpec((d, tf), lambda i, f: (0, f)),
            pl.BlockSpec((tf, d), lambda i, f: (f, 0)),
            vec, vec,
        ],
        out_specs=pl.BlockSpec((tm, d), lambda i, f: (i, 0)),
        out_shape=jax.ShapeDtypeStruct((m, d), F32),
        scratch_shapes=[pltpu.VMEM((tm, d), BF16)],
        compiler_params=_cparams("parallel", "arbitrary"),
        name="swiglu",
    )(x, wg, wu, wd, pg, pb)


def _ple_kernel(x_ref, p_ref, wg_ref, wp_ref, o_ref, xb_ref):
    j = pl.program_id(1)
    tn = o_ref.shape[1]

    @pl.when(j == 0)
    def _():
        xb_ref[...] = x_ref[...].astype(BF16)

    gate = _sigmoid(_dot(xb_ref[...], wg_ref[...]))
    xc = x_ref[:, pl.ds(pl.multiple_of(j * tn, tn), tn)]
    o_ref[...] = xc + gate * _dot(p_ref[...].astype(BF16), wp_ref[...])


def _ple(x, p, wg, wp, tm, tn):
    m, d = x.shape
    pd = p.shape[1]
    return pl.pallas_call(
        _ple_kernel,
        grid=(m // tm, d // tn),
        in_specs=[
            pl.BlockSpec((tm, d), lambda i, j: (i, 0)),
            pl.BlockSpec((tm, pd), lambda i, j: (i, 0)),
            pl.BlockSpec((d, tn), lambda i, j: (0, j)),
            pl.BlockSpec((pd, tn), lambda i, j: (0, j)),
        ],
        out_specs=pl.BlockSpec((tm, tn), lambda i, j: (i, j)),
        out_shape=jax.ShapeDtypeStruct((m, d), F32),
        scratch_shapes=[pltpu.VMEM((tm, d), BF16)],
        compiler_params=_cparams("parallel", "arbitrary"),
        name="ple",
    )(x, p, wg, wp)


def _rope_tables(pos):
    half = HEAD_DIM // 2
    inv = ROPE_THETA ** (-jnp.arange(half, dtype=F32) / half)
    ang = pos.astype(F32)[:, None] * inv[None, :]
    cos, sin = jnp.cos(ang), jnp.sin(ang)
    cos_t = jnp.concatenate([cos, cos, cos, cos], axis=-1)
    sin_t = jnp.concatenate([-sin, sin, -sin, sin], axis=-1)
    return cos_t, sin_t


def _rope_tile(h, cos, sin, first_half):
    partner = jnp.where(first_half, pltpu.roll(h, LANES - HEAD_DIM // 2, 1),
                        pltpu.roll(h, HEAD_DIM // 2, 1))
    return h * cos + partner * sin


def _proj_kernel(x_ref, wk_ref, wv_ref, wq_ref, cos_ref, sin_ref, *refs, transposed):
    if transposed:
        k_ref, v_ref, kb_ref, qt_ref, vt_ref, xb_ref = refs
    else:
        k_ref, v_ref, qb_ref, xb_ref = refs

    @pl.when(pl.program_id(1) == 0)
    def _():
        xb_ref[...] = x_ref[...].astype(BF16)

    xb = xb_ref[...]
    tm, tn = k_ref.shape
    cos, sin = cos_ref[...], sin_ref[...]
    lane = lax.broadcasted_iota(jnp.int32, (tm, LANES), 1)
    first_half = (lane % HEAD_DIM) < (HEAD_DIM // 2)
    hk = _dot(xb, wk_ref[...])
    hq = _dot(xb, wq_ref[...])
    for c in range(tn // LANES):
        cs = slice(c * LANES, (c + 1) * LANES)
        kr = _rope_tile(hk[:, cs], cos, sin, first_half)
        k_ref[:, cs] = kr
        qr = _rope_tile(hq[:, cs], cos, sin, first_half)
        if transposed:
            kb_ref[:, cs] = kr.astype(BF16)
            qt_ref[cs, :] = (qr * (ATTN_SCALE * LOG2E)).T.astype(BF16)
        else:
            qb_ref[:, cs] = (qr * ATTN_SCALE).astype(BF16)
    hv = _dot(xb, wv_ref[...])
    v_ref[...] = hv
    if transposed:
        vt_ref[...] = hv.T.astype(BF16)


def _proj(x, wk, wv, wq, cos_t, sin_t, tm, tn, transposed):
    m, d = x.shape
    wspec = pl.BlockSpec((d, tn), lambda i, j: (0, j))
    tspec = pl.BlockSpec((tm, LANES), lambda i, j: (i, 0))
    ospec = pl.BlockSpec((tm, tn), lambda i, j: (i, j))
    tr_spec = pl.BlockSpec((tn, tm), lambda i, j: (j, i))
    f32_out = jax.ShapeDtypeStruct((m, d), F32)
    if transposed:
        out_specs = [ospec, ospec, ospec, tr_spec, tr_spec]
        out_shape = [f32_out, f32_out, jax.ShapeDtypeStruct((m, d), BF16),
                     jax.ShapeDtypeStruct((d, m), BF16), jax.ShapeDtypeStruct((d, m), BF16)]
    else:
        out_specs = [ospec, ospec, ospec]
        out_shape = [f32_out, f32_out, jax.ShapeDtypeStruct((m, d), BF16)]
    return pl.pallas_call(
        functools.partial(_proj_kernel, transposed=transposed),
        grid=(m // tm, d // tn),
        in_specs=[pl.BlockSpec((tm, d), lambda i, j: (i, 0)), wspec, wspec, wspec, tspec, tspec],
        out_specs=out_specs,
        out_shape=out_shape,
        scratch_shapes=[pltpu.VMEM((tm, d), BF16)],
        compiler_params=_cparams("parallel", "arbitrary"),
        name="proj",
    )(x, wk, wv, wq, cos_t, sin_t)


def _diff_lambda(lam_ref, lam_init):
    lp = lam_ref[...]
    s1 = jnp.sum(lp[0:1, :] * lp[1:2, :], axis=-1, keepdims=True)
    s2 = jnp.sum(lp[2:3, :] * lp[3:4, :], axis=-1, keepdims=True)
    return jnp.exp(s1) - jnp.exp(s2) + lam_init


def _diff_finalize(o1, o2, lam, g, lam_init):
    o = o1 - lam * o2
    o = o * lax.rsqrt(jnp.mean(o * o, axis=-1, keepdims=True) + LN_EPS)
    return o * g * (1.0 - lam_init)


def _attn_prompt_kernel(qt_ref, k_ref, vt_ref, lam_ref, g_ref, o_ref,
                        sa_ref, sb_ref, m_ref, l_ref, acc_ref, *, lam_init, tk):
    tq = qt_ref.shape[1]
    i = pl.program_id(1)
    q0 = i * tq
    n_full = q0 // tk
    qt = qt_ref[...]
    feat = lax.broadcasted_iota(jnp.int32, qt.shape, 0)
    qz = jnp.concatenate([jnp.where(feat < HEAD_DIM, qt, 0), jnp.where(feat >= HEAD_DIM, qt, 0)], axis=1)
    m_ref[...] = jnp.full(m_ref.shape, -jnp.inf, F32)
    l_ref[...] = jnp.zeros(l_ref.shape, F32)
    acc_ref[...] = jnp.zeros(acc_ref.shape, F32)

    def scores(b, dst_ref):
        base = pl.multiple_of(b * tk, tk)
        dst_ref[...] = _dot(k_ref[pl.ds(base, tk), :], qz)

    def softmax_pv(b, src_ref, masked):
        base = pl.multiple_of(b * tk, tk)
        s = src_ref[...]
        if masked:
            key = base + lax.broadcasted_iota(jnp.int32, s.shape, 0)
            qry = q0 + lax.broadcasted_iota(jnp.int32, s.shape, 1) % tq
            s = jnp.where(key <= qry, s, -jnp.inf)
        m_old = m_ref[...]
        m_new = jnp.maximum(m_old, jnp.max(s, axis=0, keepdims=True))
        a = jnp.exp2(m_old - m_new)
        p = jnp.exp2(s - m_new)
        l_ref[...] = a * l_ref[...] + jnp.sum(p, axis=0, keepdims=True)
        m_ref[...] = m_new
        acc_ref[...] = a * acc_ref[...] + _dot(vt_ref[:, pl.ds(base, tk)], p.astype(BF16))

    scores(0, sa_ref)

    def pair(t, carry):
        b = 2 * t
        scores(b + 1, sb_ref)
        softmax_pv(b, sa_ref, False)
        scores(b + 2, sa_ref)
        softmax_pv(b + 1, sb_ref, False)
        return carry

    lax.fori_loop(0, n_full // 2, pair, 0)

    @pl.when(n_full % 2 == 1)
    def _():
        scores(n_full, sb_ref)
        softmax_pv(n_full - 1, sa_ref, False)
        softmax_pv(n_full, sb_ref, True)

    @pl.when(n_full % 2 == 0)
    def _():
        softmax_pv(n_full, sa_ref, True)

    lam = _diff_lambda(lam_ref, lam_init)
    o = acc_ref[...] * (1.0 / l_ref[...])
    ot = o[:, :tq] - lam * o[:, tq:]
    ot = ot * lax.rsqrt(jnp.mean(ot * ot, axis=0, keepdims=True) + LN_EPS)
    ot = ot * g_ref[...] * (1.0 - lam_init)
    o_ref[...] = ot.T.astype(o_ref.dtype)


def _attn_prompt(qt, kb, vt, lam_p, sub_g_col, lam_init, tq, tk):
    d, s = qt.shape
    nh = d // V_DIM
    return pl.pallas_call(
        functools.partial(_attn_prompt_kernel, lam_init=lam_init, tk=tk),
        grid=(nh, s // tq),
        in_specs=[
            pl.BlockSpec((V_DIM, tq), lambda h, i: (h, i)),
            pl.BlockSpec((s, V_DIM), lambda h, i: (0, h)),
            pl.BlockSpec((V_DIM, s), lambda h, i: (h, 0)),
            pl.BlockSpec(lam_p.shape, lambda h, i: (0, 0)),
            pl.BlockSpec((V_DIM, 1), lambda h, i: (0, 0)),
        ],
        out_specs=pl.BlockSpec((tq, V_DIM), lambda h, i: (i, h)),
        out_shape=jax.ShapeDtypeStruct((s, d), BF16),
        scratch_shapes=[pltpu.VMEM((tk, 2 * tq), F32), pltpu.VMEM((tk, 2 * tq), F32),
                        pltpu.VMEM((1, 2 * tq), F32), pltpu.VMEM((1, 2 * tq), F32),
                        pltpu.VMEM((V_DIM, 2 * tq), F32)],
        compiler_params=_cparams("parallel", "arbitrary"),
        name="attn_prompt",
    )(qt, kb, vt, lam_p, sub_g_col)


def _attn_sample_kernel(pt_ref, qk_ref, vn_ref, lam_ref, g_ref, e_ref, *refs,
                        lam_init, pages_per_step):
    k_refs = refs[:pages_per_step]
    v_refs = refs[pages_per_step:2 * pages_per_step]
    o_ref = refs[2 * pages_per_step]
    qb_ref, m_ref, l_ref, acc_ref = refs[2 * pages_per_step + 1:]
    step_idx = pl.program_id(1)
    n_maps, _, page = qb_ref.shape
    n_heads = n_maps // 2

    @pl.when(step_idx == 0)
    def _():
        qk = qk_ref[0]
        qt = qk.T
        for mp in range(n_maps):
            qb_ref[mp] = jnp.broadcast_to(qt[:HEAD_DIM, mp:mp + 1], (HEAD_DIM, page))
        m_ref[...] = jnp.sum(qk[:, :HEAD_DIM] * qk[:, HEAD_DIM:], axis=-1, keepdims=True)
        l_ref[...] = jnp.ones_like(l_ref)
        vn = vn_ref[0]
        for h in range(n_heads):
            acc_ref[2 * h:2 * h + 2, :] = jnp.broadcast_to(vn[h:h + 1, :], (2, V_DIM))

    slot = lax.broadcasted_iota(jnp.int32, (n_maps, page * n_heads), 1)
    mp_row = lax.broadcasted_iota(jnp.int32, (n_maps, page * n_heads), 0)
    own_head = (slot % n_heads) == (mp_row >> 1)
    for kp_ref, vp_ref in zip(k_refs, v_refs):
        s = jnp.sum(kp_ref[0] * qb_ref[...], axis=1)
        m_old = m_ref[...]
        m_new = jnp.maximum(m_old, jnp.max(s, axis=-1, keepdims=True))
        a = jnp.exp(m_old - m_new)
        p = jnp.exp(s - m_new)
        l_ref[...] = a * l_ref[...] + jnp.sum(p, axis=-1, keepdims=True)
        m_ref[...] = m_new
        pe = _dot(p.astype(BF16), e_ref[...])
        pm = jnp.where(own_head, pe, 0.0).astype(BF16)
        vb = vp_ref[0].reshape(page * n_heads, V_DIM).astype(BF16)
        acc_ref[...] = a * acc_ref[...] + _dot(pm, vb)

    @pl.when(step_idx == pl.num_programs(1) - 1)
    def _():
        acc_ref[...] = acc_ref[...] / l_ref[...]
        o1 = acc_ref[pl.ds(0, n_heads, stride=2), :]
        o2 = acc_ref[pl.ds(1, n_heads, stride=2), :]
        lam = _diff_lambda(lam_ref, lam_init)
        o_ref[0] = _diff_finalize(o1, o2, lam, g_ref[...], lam_init).astype(o_ref.dtype)


def _attn_sample(page_table, qk3, vn3, lam_p, sub_g, cache_kt, cache_v, lam_init, pages_per_step):
    nb, n_pages = page_table.shape
    _, n_maps, hd, page = cache_kt.shape
    n_heads = cache_v.shape[2]
    pps = pages_per_step

    def kspec(j):
        return pl.BlockSpec((1, n_maps, hd, page), lambda b, s, pt: (pt[b, s * pps + j], 0, 0, 0))

    def vspec(j):
        return pl.BlockSpec((1, page, n_heads, V_DIM), lambda b, s, pt: (pt[b, s * pps + j], 0, 0, 0))

    expand = (jnp.arange(page * n_heads)[None, :] // n_heads == jnp.arange(page)[:, None]).astype(BF16)
    grid_spec = pltpu.PrefetchScalarGridSpec(
        num_scalar_prefetch=1,
        grid=(nb, n_pages // pps),
        in_specs=[
            pl.BlockSpec((1, n_maps, 2 * hd), lambda b, s, pt: (b, 0, 0)),
            pl.BlockSpec((1, n_heads, V_DIM), lambda b, s, pt: (b, 0, 0)),
            pl.BlockSpec(lam_p.shape, lambda b, s, pt: (0, 0)),
            pl.BlockSpec((1, V_DIM), lambda b, s, pt: (0, 0)),
            pl.BlockSpec(expand.shape, lambda b, s, pt: (0, 0)),
        ] + [kspec(j) for j in range(pps)] + [vspec(j) for j in range(pps)],
        out_specs=pl.BlockSpec((1, n_heads, V_DIM), lambda b, s, pt: (b, 0, 0)),
        scratch_shapes=[
            pltpu.VMEM((n_maps, hd, page), F32),
            pltpu.VMEM((n_maps, 1), F32),
            pltpu.VMEM((n_maps, 1), F32),
            pltpu.VMEM((n_maps, V_DIM), F32),
        ],
    )
    return pl.pallas_call(
        functools.partial(_attn_sample_kernel, lam_init=lam_init, pages_per_step=pps),
        grid_spec=grid_spec,
        out_shape=jax.ShapeDtypeStruct((nb, n_heads, V_DIM), BF16),
        compiler_params=_cparams("parallel", "arbitrary"),
        name="attn_sample",
    )(page_table, qk3, vn3, lam_p, sub_g, expand, *([cache_kt] * pps), *([cache_v] * pps))


def _wo_kernel(a_ref, x_ref, wo_ref, pg_ref, pb_ref, o_ref, *, alpha):
    m = _dot(a_ref[...], wo_ref[...])
    o_ref[...] = _ln(alpha * x_ref[...] + m, pg_ref[...], pb_ref[...])


def _wo(a, x, wo, pg, pb, alpha, tm):
    m, d = x.shape
    vec = pl.BlockSpec((1, d), lambda i: (0, 0))
    return pl.pallas_call(
        functools.partial(_wo_kernel, alpha=alpha),
        grid=(m // tm,),
        in_specs=[
            pl.BlockSpec((tm, d), lambda i: (i, 0)),
            pl.BlockSpec((tm, d), lambda i: (i, 0)),
            pl.BlockSpec((d, d), lambda i: (0, 0)),
            vec, vec,
        ],
        out_specs=pl.BlockSpec((tm, d), lambda i: (i, 0)),
        out_shape=jax.ShapeDtypeStruct((m, d), F32),
        compiler_params=_cparams("parallel"),
        name="wo",
    )(a, x, wo, pg, pb)


def _moe_kernel(x_ref, wr_ref, wg_ref, wu_ref, wd_ref, pg_ref, pb_ref, o_ref, xb_ref, comb_ref,
                *, alpha):
    e = pl.program_id(1)
    n_exp = wr_ref.shape[1]

    @pl.when(e == 0)
    def _():
        x = x_ref[...]
        xb_ref[...] = x.astype(BF16)
        logits = jnp.dot(x, wr_ref[...], preferred_element_type=F32,
                         precision=lax.Precision.HIGHEST)
        idx = lax.broadcasted_iota(jnp.int32, logits.shape, 1)
        v1 = jnp.max(logits, axis=-1, keepdims=True)
        i1 = jnp.min(jnp.where(logits == v1, idx, n_exp), axis=-1, keepdims=True)
        rest = jnp.where(idx == i1, -jnp.inf, logits)
        v2 = jnp.max(rest, axis=-1, keepdims=True)
        i2 = jnp.min(jnp.where(rest == v2, idx, n_exp), axis=-1, keepdims=True)
        t = jnp.exp(v2 - v1)
        g1 = 1.0 / (1.0 + t)
        g2 = t / (1.0 + t)
        comb_ref[...] = jnp.where(idx == i1, g1, 0.0) + jnp.where(idx == i2, g2, 0.0)

    xb = xb_ref[...]
    comb = comb_ref[...]
    idx = lax.broadcasted_iota(jnp.int32, comb.shape, 1)
    w = jnp.sum(jnp.where(idx == e, comb, 0.0), axis=-1, keepdims=True)
    h = _silu(_dot(xb, wg_ref[0])) * _dot(xb, wu_ref[0])
    part = _dot((h * w).astype(BF16), wd_ref[0])

    @pl.when(e == 0)
    def _():
        o_ref[...] = part

    @pl.when(e > 0)
    def _():
        o_ref[...] += part

    @pl.when(e == pl.num_programs(1) - 1)
    def _():
        o_ref[...] = _ln(alpha * x_ref[...] + o_ref[...], pg_ref[...], pb_ref[...])


def _moe(x, wr, wg, wu, wd, pg, pb, alpha, tm):
    m, d = x.shape
    n_exp, _, dexp = wg.shape
    vec = pl.BlockSpec((1, d), lambda i, e: (0, 0))
    return pl.pallas_call(
        functools.partial(_moe_kernel, alpha=alpha),
        grid=(m // tm, n_exp),
        in_specs=[
            pl.BlockSpec((tm, d), lambda i, e: (i, 0)),
            pl.BlockSpec((d, n_exp), lambda i, e: (0, 0)),
            pl.BlockSpec((1, d, dexp), lambda i, e: (e, 0, 0)),
            pl.BlockSpec((1, d, dexp), lambda i, e: (e, 0, 0)),
            pl.BlockSpec((1, dexp, d), lambda i, e: (e, 0, 0)),
            vec, vec,
        ],
        out_specs=pl.BlockSpec((tm, d), lambda i, e: (i, 0)),
        out_shape=jax.ShapeDtypeStruct((m, d), F32),
        scratch_shapes=[pltpu.VMEM((tm, d), BF16), pltpu.VMEM((tm, n_exp), F32)],
        compiler_params=_cparams("parallel", "arbitrary"),
        name="moe",
    )(x, wr, wg, wu, wd, pg, pb)


def kernel(x_prompt, x_sample, state_conv, cache_k, cache_v, page_table, p_prompt, p_sample,
           conv_w_pw1, conv_b_pw1, conv_w_dw, conv_b_dw, conv_ln_g, conv_ln_b, conv_w_pw2,
           kv_w_k, kv_w_v, attn_w_q, attn_lambda, attn_sub_g, attn_w_o,
           ffn_w_gate, ffn_w_up, ffn_w_down, moe_w_router, moe_w_gate, moe_w_up, moe_w_down,
           post_ln_g, post_ln_b, ple_w_proj, ple_w_gate):
    bp, sp, d = x_prompt.shape
    bd, sd, _ = x_sample.shape
    depth = post_ln_g.shape[0]
    n_a = conv_w_pw1.shape[0]
    assert bp == 1 and sd == 1 and depth == 2 and n_a == 1
    n_heads = d // V_DIM
    n_maps = 2 * n_heads
    past = page_table.shape[1] * cache_k.shape[1]
    alpha = (2.0 * depth) ** 0.25
    lam_init = 0.8 - 0.6 * math.exp(-0.3 * 1)

    bf = lambda w: w.astype(BF16)
    row = lambda v: v.reshape(1, -1)

    xp = x_prompt.reshape(sp, d)
    xd = x_sample.reshape(bd, d)
    pp = p_prompt.reshape(depth, sp, -1)
    pdm = p_sample.reshape(depth, bd, -1)

    w1, w2 = bf(conv_w_pw1[0]), bf(conv_w_pw2[0])
    b1 = row(conv_b_pw1[0])
    conv_args = (conv_w_dw[0], row(conv_b_dw[0]), row(conv_ln_g[0]), row(conv_ln_b[0]), w2,
                 row(post_ln_g[0, 0]), row(post_ln_b[0, 0]), alpha)
    up = _glu(xp, w1, b1, 512, 512)
    ud = _glu(xd, w1, b1, bd, 512)
    conv_prompt = up[sp - (CONV_W - 1):].reshape(1, 1, CONV_W - 1, d)
    hist = jnp.transpose(state_conv[0], (1, 0, 2))
    conv_sample = jnp.transpose(jnp.concatenate([hist[1:], ud[None]], axis=0), (1, 0, 2))[None]
    xp = _conv_prompt(up, xp, *conv_args, 256)
    xd = _conv_sample(hist, ud, xd, *conv_args)

    fw = (bf(ffn_w_gate[0]), bf(ffn_w_up[0]), bf(ffn_w_down[0]), row(post_ln_g[0, 1]), row(post_ln_b[0, 1]), alpha)
    xp = _swiglu(xp, *fw, 512, 512)
    xd = _swiglu(xd, *fw, bd, 512)

    pw = (bf(ple_w_gate[0]), bf(ple_w_proj[0]))
    xp = _ple(xp, pp[0], *pw, 1024, 512)
    xd = _ple(xd, pdm[0], *pw, bd, 512)

    wk, wv, wq = bf(kv_w_k), bf(kv_w_v), bf(attn_w_q[0])
    cos_p, sin_p = _rope_tables(jnp.arange(sp))
    cos_d, sin_d = _rope_tables(jnp.full((bd,), past))
    k_p, v_p, kb_p, qt_p, vt_p = _proj(xp, wk, wv, wq, cos_p, sin_p, 512, 512, True)
    k_d, v_d, qb_d = _proj(xd, wk, wv, wq, cos_d, sin_d, bd, 512, False)

    lam_p = attn_lambda[0]
    sub_g = row(attn_sub_g[0])
    ap = _attn_prompt(qt_p, kb_p, vt_p, lam_p, attn_sub_g[0].reshape(V_DIM, 1), lam_init, 512, 512)
    cache_kt = jnp.transpose(cache_k, (0, 2, 3, 1))
    qk3 = jnp.concatenate([qb_d.astype(F32).reshape(bd, n_maps, HEAD_DIM),
                           k_d.astype(BF16).astype(F32).reshape(bd, n_maps, HEAD_DIM)], axis=-1)
    ad = _attn_sample(page_table, qk3, v_d.reshape(bd, n_heads, V_DIM), lam_p, sub_g,
                      cache_kt, cache_v, lam_init, 8).reshape(bd, d)
    wo_args = (bf(attn_w_o[0]), row(post_ln_g[1, 0]), row(post_ln_b[1, 0]), alpha)
    xp = _wo(ap, xp, *wo_args, 256)
    xd = _wo(ad, xd, *wo_args, bd)

    mw = (moe_w_router[0], bf(moe_w_gate[0]), bf(moe_w_up[0]), bf(moe_w_down[0]),
          row(post_ln_g[1, 1]), row(post_ln_b[1, 1]), alpha)
    xp = _moe(xp, *mw, 256)
    xd = _moe(xd, *mw, bd)

    pw = (bf(ple_w_gate[1]), bf(ple_w_proj[1]))
    xp = _ple(xp, pp[1], *pw, 1024, 512)
    xd = _ple(xd, pdm[1], *pw, bd, 512)

    return (xp.reshape(bp, sp, d), xd.reshape(bd, sd, d), conv_prompt, conv_sample,
            k_p.reshape(bp, sp, n_maps, HEAD_DIM), v_p.reshape(bp, sp, n_heads, V_DIM),
            k_d.reshape(bd, sd, n_maps, HEAD_DIM), v_d.reshape(bd, sd, n_heads, V_DIM))
```

```python
import functools
import math

import jax
import jax.numpy as jnp
from jax import lax
from jax.experimental import pallas as pl
from jax.experimental.pallas import tpu as pltpu

HEAD_DIM = 64
V_DIM = 2 * HEAD_DIM
CONV_W = 31
TOP_K = 2
ROPE_THETA = 10000.0
LN_EPS = 1e-5
ATTN_SCALE = HEAD_DIM ** -0.5
LOG2E = math.log2(math.e)

LANES = 128
SUBLANES = 8
MOE_SUB_ROWS = 256
HALO = 32
VMEM_LIMIT = 56 * 1024 * 1024

F32 = jnp.float32
BF16 = jnp.bfloat16


def _cparams(*sem):
    return pltpu.CompilerParams(dimension_semantics=sem, vmem_limit_bytes=VMEM_LIMIT)


def _dot(a, b):
    return jnp.dot(a, b, preferred_element_type=F32)


def _ln(x, g, b):
    mu = jnp.mean(x, axis=-1, keepdims=True)
    xc = x - mu
    var = jnp.mean(xc * xc, axis=-1, keepdims=True)
    return xc * lax.rsqrt(var + LN_EPS) * g + b


def _sigmoid(x):
    return 1.0 / (1.0 + jnp.exp(-x))


def _silu(x):
    return x * _sigmoid(x)


def _glu_kernel(x_ref, wa_ref, wg_ref, ba_ref, bg_ref, u_ref, xb_ref):
    @pl.when(pl.program_id(1) == 0)
    def _():
        xb_ref[...] = x_ref[...].astype(BF16)

    xb = xb_ref[...]
    a = _dot(xb, wa_ref[...]) + ba_ref[...]
    g = _dot(xb, wg_ref[...]) + bg_ref[...]
    u_ref[...] = a * _sigmoid(g)


def _glu(x, w1, b1, tm, tn):
    m, d = x.shape
    nj = d // tn
    return pl.pallas_call(
        _glu_kernel,
        grid=(m // tm, nj),
        in_specs=[
            pl.BlockSpec((tm, d), lambda i, j: (i, 0)),
            pl.BlockSpec((d, tn), lambda i, j: (0, j)),
            pl.BlockSpec((d, tn), lambda i, j: (0, j + nj)),
            pl.BlockSpec((1, tn), lambda i, j: (0, j)),
            pl.BlockSpec((1, tn), lambda i, j: (0, j + nj)),
        ],
        out_specs=pl.BlockSpec((tm, tn), lambda i, j: (i, j)),
        out_shape=jax.ShapeDtypeStruct((m, d), F32),
        scratch_shapes=[pltpu.VMEM((tm, d), BF16)],
        compiler_params=_cparams("parallel", "arbitrary"),
        name="glu",
    )(x, w1, w1, b1, b1)


def _mix_tail(y, x, g_ref, b_ref, w2_ref, pg_ref, pb_ref, alpha):
    y = _silu(_ln(y, g_ref[...], b_ref[...]))
    m = _dot(y.astype(BF16), w2_ref[...])
    return _ln(alpha * x + m, pg_ref[...], pb_ref[...])


def _conv_prompt_kernel(uprev_ref, u_ref, x_ref, wdw_ref, bdw_ref, g_ref, b_ref, w2_ref,
                        pg_ref, pb_ref, o_ref, full_ref, y_ref, *, alpha, rows):
    tm, d = u_ref.shape
    first = pl.program_id(0) == 0
    full_ref[0:HALO, :] = jnp.where(first, 0.0, uprev_ref[...])
    full_ref[HALO:, :] = u_ref[...]
    off = HALO - (CONV_W - 1)

    def col_body(c, carry):
        cs = pl.ds(pl.multiple_of(c * LANES, LANES), LANES)
        w = wdw_ref[:, cs]
        bias = bdw_ref[:, cs]

        def row_body(r, carry2):
            base = pl.multiple_of(r * rows, rows)
            win = full_ref[pl.ds(base, rows + HALO), cs]
            acc = jnp.zeros((rows, LANES), F32)
            for res in range(SUBLANES):
                sh = win if res == 0 else pltpu.roll(win, rows + HALO - res, 0)
                for t in range(CONV_W):
                    if (off + t) % SUBLANES == res:
                        a8 = off + t - res
                        acc = acc + sh[a8:a8 + rows, :] * w[t:t + 1, :]
            y_ref[pl.ds(base, rows), cs] = acc + bias
            return carry2

        return lax.fori_loop(0, tm // rows, row_body, carry)

    lax.fori_loop(0, d // LANES, col_body, 0)
    o_ref[...] = _mix_tail(y_ref[...], x_ref[...], g_ref, b_ref, w2_ref, pg_ref, pb_ref, alpha)


def _conv_prompt(u, x, wdw, bdw, g, b, w2, pg, pb, alpha, tm):
    m, d = x.shape
    nh = tm // HALO
    vec = pl.BlockSpec((1, d), lambda i: (0, 0))
    return pl.pallas_call(
        functools.partial(_conv_prompt_kernel, alpha=alpha, rows=32),
        grid=(m // tm,),
        in_specs=[
            pl.BlockSpec((HALO, d), lambda i: (jnp.maximum(i * nh - 1, 0), 0)),
            pl.BlockSpec((tm, d), lambda i: (i, 0)),
            pl.BlockSpec((tm, d), lambda i: (i, 0)),
            pl.BlockSpec((CONV_W, d), lambda i: (0, 0)),
            vec, vec, vec,
            pl.BlockSpec((d, d), lambda i: (0, 0)),
            vec, vec,
        ],
        out_specs=pl.BlockSpec((tm, d), lambda i: (i, 0)),
        out_shape=jax.ShapeDtypeStruct((m, d), F32),
        scratch_shapes=[pltpu.VMEM((tm + HALO, d), F32), pltpu.VMEM((tm, d), F32)],
        compiler_params=_cparams("parallel"),
        name="conv_prompt",
    )(u, u, x, wdw, bdw, g, b, w2, pg, pb)


def _conv_sample_kernel(hist_ref, u_ref, x_ref, wdw_ref, bdw_ref, g_ref, b_ref, w2_ref,
                        pg_ref, pb_ref, o_ref, *, alpha):
    y = u_ref[...] * wdw_ref[CONV_W - 1:CONV_W, :]
    for t in range(CONV_W - 1):
        y = y + hist_ref[t] * wdw_ref[t:t + 1, :]
    y = y + bdw_ref[...]
    o_ref[...] = _mix_tail(y, x_ref[...], g_ref, b_ref, w2_ref, pg_ref, pb_ref, alpha)


def _conv_sample(hist, u, x, wdw, bdw, g, b, w2, pg, pb, alpha):
    m, d = x.shape
    return pl.pallas_call(
        functools.partial(_conv_sample_kernel, alpha=alpha),
        out_shape=jax.ShapeDtypeStruct((m, d), F32),
        compiler_params=pltpu.CompilerParams(vmem_limit_bytes=VMEM_LIMIT),
        name="conv_sample",
    )(hist, u, x, wdw, bdw, g, b, w2, pg, pb)


def _swiglu_kernel(x_ref, wg_ref, wu_ref, wd_ref, pg_ref, pb_ref, o_ref, xb_ref, *, alpha):
    f = pl.program_id(1)

    @pl.when(f == 0)
    def _():
        xb_ref[...] = x_ref[...].astype(BF16)

    xb = xb_ref[...]
    h = _silu(_dot(xb, wg_ref[...])) * _dot(xb, wu_ref[...])
    part = _dot(h.astype(BF16), wd_ref[...])

    @pl.when(f == 0)
    def _():
        o_ref[...] = part

    @pl.when(f > 0)
    def _():
        o_ref[...] += part

    @pl.when(f == pl.num_programs(1) - 1)
    def _():
        o_ref[...] = _ln(alpha * x_ref[...] + o_ref[...], pg_ref[...], pb_ref[...])


def _swiglu(x, wg, wu, wd, pg, pb, alpha, tm, tf):
    m, d = x.shape
    dff = wg.shape[1]
    vec = pl.BlockSpec((1, d), lambda i, f: (0, 0))
    return pl.pallas_call(
        functools.partial(_swiglu_kernel, alpha=alpha),
        grid=(m // tm, dff // tf),
        in_specs=[
            pl.BlockSpec((tm, d), lambda i, f: (i, 0)),
            pl.BlockSpec((d, tf), lambda i, f: (0, f)),
            pl.BlockSpec((d, tf), lambda i, f: (0, f)),
            pl.BlockSpec((tf, d), lambda i, f: (f, 0)),
            vec, vec,
        ],
        out_specs=pl.BlockSpec((tm, d), lambda i, f: (i, 0)),
        out_shape=jax.ShapeDtypeStruct((m, d), F32),
        scratch_shapes=[pltpu.VMEM((tm, d), BF16)],
        compiler_params=_cparams("parallel", "arbitrary"),
        name="swiglu",
    )(x, wg, wu, wd, pg, pb)


def _ple_kernel(*refs, alpha, post_ln):
    if post_ln:
        x_ref, f_ref, pg_ref, pb_ref, p_ref, wg_ref, wp_ref, o_ref, xb_ref, xf_ref = refs
    else:
        x_ref, p_ref, wg_ref, wp_ref, o_ref, xb_ref = refs
        xf_ref = x_ref
    j = pl.program_id(1)
    tn = o_ref.shape[1]

    @pl.when(j == 0)
    def _():
        if post_ln:
            xf_ref[...] = _ln(alpha * x_ref[...] + f_ref[...], pg_ref[...], pb_ref[...])
        xb_ref[...] = xf_ref[...].astype(BF16)

    gate = _sigmoid(_dot(xb_ref[...], wg_ref[...]))
    xc = xf_ref[:, pl.ds(pl.multiple_of(j * tn, tn), tn)]
    o_ref[...] = xc + gate * _dot(p_ref[...].astype(BF16), wp_ref[...])


def _ple(x, p, wg, wp, tm, tn, post=None):
    m, d = x.shape
    pd = p.shape[1]
    row = pl.BlockSpec((tm, d), lambda i, j: (i, 0))
    vec = pl.BlockSpec((1, d), lambda i, j: (0, 0))
    tail_specs = [pl.BlockSpec((tm, pd), lambda i, j: (i, 0)),
                  pl.BlockSpec((d, tn), lambda i, j: (0, j)),
                  pl.BlockSpec((pd, tn), lambda i, j: (0, j))]
    scratch = [pltpu.VMEM((tm, d), BF16)]
    if post is None:
        args, in_specs, alpha = (x, p, wg, wp), [row] + tail_specs, 1.0
    else:
        f, pg, pb, alpha = post
        args, in_specs = (x, f, pg, pb, p, wg, wp), [row, row, vec, vec] + tail_specs
        scratch.append(pltpu.VMEM((tm, d), F32))
    return pl.pallas_call(
        functools.partial(_ple_kernel, alpha=alpha, post_ln=post is not None),
        grid=(m // tm, d // tn),
        in_specs=in_specs,
        out_specs=pl.BlockSpec((tm, tn), lambda i, j: (i, j)),
        out_shape=jax.ShapeDtypeStruct((m, d), F32),
        scratch_shapes=scratch,
        compiler_params=_cparams("parallel", "arbitrary"),
        name="ple",
    )(*args)


def _rope_tables(pos):
    half = HEAD_DIM // 2
    inv = ROPE_THETA ** (-jnp.arange(half, dtype=F32) / half)
    ang = pos.astype(F32)[:, None] * inv[None, :]
    cos, sin = jnp.cos(ang), jnp.sin(ang)
    cos_t = jnp.concatenate([cos, cos, cos, cos], axis=-1)
    sin_t = jnp.concatenate([-sin, sin, -sin, sin], axis=-1)
    return cos_t, sin_t


def _rope_tile(h, cos, sin, first_half):
    partner = jnp.where(first_half, pltpu.roll(h, LANES - HEAD_DIM // 2, 1),
                        pltpu.roll(h, HEAD_DIM // 2, 1))
    return h * cos + partner * sin


def _proj_kernel(x_ref, wk_ref, wv_ref, wq_ref, cos_ref, sin_ref, *refs, transposed):
    if transposed:
        k_ref, v_ref, kb_ref, qt_ref, vt_ref, xb_ref = refs
    else:
        k_ref, v_ref, qb_ref, xb_ref = refs

    @pl.when(pl.program_id(1) == 0)
    def _():
        xb_ref[...] = x_ref[...].astype(BF16)

    xb = xb_ref[...]
    tm, tn = k_ref.shape
    cos, sin = cos_ref[...], sin_ref[...]
    lane = lax.broadcasted_iota(jnp.int32, (tm, LANES), 1)
    first_half = (lane % HEAD_DIM) < (HEAD_DIM // 2)
    hk = _dot(xb, wk_ref[...])
    hq = _dot(xb, wq_ref[...])
    for c in range(tn // LANES):
        cs = slice(c * LANES, (c + 1) * LANES)
        kr = _rope_tile(hk[:, cs], cos, sin, first_half)
        k_ref[:, cs] = kr
        qr = _rope_tile(hq[:, cs], cos, sin, first_half)
        if transposed:
            kb_ref[:, cs] = kr.astype(BF16)
            qt_ref[cs, :] = (qr * (ATTN_SCALE * LOG2E)).T.astype(BF16)
        else:
            qb_ref[:, cs] = (qr * ATTN_SCALE).astype(BF16)
    hv = _dot(xb, wv_ref[...])
    v_ref[...] = hv
    if transposed:
        vt_ref[...] = hv.T.astype(BF16)


def _proj(x, wk, wv, wq, cos_t, sin_t, tm, tn, transposed):
    m, d = x.shape
    wspec = pl.BlockSpec((d, tn), lambda i, j: (0, j))
    tspec = pl.BlockSpec((tm, LANES), lambda i, j: (i, 0))
    ospec = pl.BlockSpec((tm, tn), lambda i, j: (i, j))
    tr_spec = pl.BlockSpec((tn, tm), lambda i, j: (j, i))
    f32_out = jax.ShapeDtypeStruct((m, d), F32)
    if transposed:
        out_specs = [ospec, ospec, ospec, tr_spec, tr_spec]
        out_shape = [f32_out, f32_out, jax.ShapeDtypeStruct((m, d), BF16),
                     jax.ShapeDtypeStruct((d, m), BF16), jax.ShapeDtypeStruct((d, m), BF16)]
    else:
        out_specs = [ospec, ospec, ospec]
        out_shape = [f32_out, f32_out, jax.ShapeDtypeStruct((m, d), BF16)]
    return pl.pallas_call(
        functools.partial(_proj_kernel, transposed=transposed),
        grid=(m // tm, d // tn),
        in_specs=[pl.BlockSpec((tm, d), lambda i, j: (i, 0)), wspec, wspec, wspec, tspec, tspec],
        out_specs=out_specs,
        out_shape=out_shape,
        scratch_shapes=[pltpu.VMEM((tm, d), BF16)],
        compiler_params=_cparams("parallel", "arbitrary"),
        name="proj",
    )(x, wk, wv, wq, cos_t, sin_t)


def _diff_lambda(lam_ref, lam_init):
    lp = lam_ref[...]
    s1 = jnp.sum(lp[0:1, :] * lp[1:2, :], axis=-1, keepdims=True)
    s2 = jnp.sum(lp[2:3, :] * lp[3:4, :], axis=-1, keepdims=True)
    return jnp.exp(s1) - jnp.exp(s2) + lam_init


def _diff_finalize(o1, o2, lam, g, lam_init):
    o = o1 - lam * o2
    o = o * lax.rsqrt(jnp.mean(o * o, axis=-1, keepdims=True) + LN_EPS)
    return o * g * (1.0 - lam_init)


def _attn_prompt_kernel(qt_ref, k_ref, vt_ref, lam_ref, g_ref, o_ref,
                        sa_ref, sb_ref, m_ref, l_ref, acc_ref, *, lam_init, tk):
    tq = qt_ref.shape[1]
    i = pl.program_id(1)
    q0 = i * tq
    n_full = q0 // tk
    qt = qt_ref[...]
    feat = lax.broadcasted_iota(jnp.int32, qt.shape, 0)
    qz = jnp.concatenate([jnp.where(feat < HEAD_DIM, qt, 0), jnp.where(feat >= HEAD_DIM, qt, 0)], axis=1)
    m_ref[...] = jnp.full(m_ref.shape, -jnp.inf, F32)
    l_ref[...] = jnp.zeros(l_ref.shape, F32)
    acc_ref[...] = jnp.zeros(acc_ref.shape, F32)

    def scores(b, dst_ref):
        base = pl.multiple_of(b * tk, tk)
        dst_ref[...] = _dot(k_ref[pl.ds(base, tk), :], qz)

    def softmax_pv(b, src_ref, masked):
        base = pl.multiple_of(b * tk, tk)
        s = src_ref[...]
        if masked:
            key = base + lax.broadcasted_iota(jnp.int32, s.shape, 0)
            qry = q0 + lax.broadcasted_iota(jnp.int32, s.shape, 1) % tq
            s = jnp.where(key <= qry, s, -jnp.inf)
        m_old = m_ref[...]
        m_new = jnp.maximum(m_old, jnp.max(s, axis=0, keepdims=True))
        a = jnp.exp2(m_old - m_new)
        p = jnp.exp2(s - m_new)
        l_ref[...] = a * l_ref[...] + jnp.sum(p, axis=0, keepdims=True)
        m_ref[...] = m_new
        acc_ref[...] = a * acc_ref[...] + _dot(vt_ref[:, pl.ds(base, tk)], p.astype(BF16))

    scores(0, sa_ref)

    def pair(t, carry):
        b = 2 * t
        scores(b + 1, sb_ref)
        softmax_pv(b, sa_ref, False)
        scores(b + 2, sa_ref)
        softmax_pv(b + 1, sb_ref, False)
        return carry

    lax.fori_loop(0, n_full // 2, pair, 0)

    @pl.when(n_full % 2 == 1)
    def _():
        scores(n_full, sb_ref)
        softmax_pv(n_full - 1, sa_ref, False)
        softmax_pv(n_full, sb_ref, True)

    @pl.when(n_full % 2 == 0)
    def _():
        softmax_pv(n_full, sa_ref, True)

    lam = _diff_lambda(lam_ref, lam_init)
    o = acc_ref[...] * (1.0 / l_ref[...])
    ot = o[:, :tq] - lam * o[:, tq:]
    ot = ot * lax.rsqrt(jnp.mean(ot * ot, axis=0, keepdims=True) + LN_EPS)
    ot = ot * g_ref[...] * (1.0 - lam_init)
    o_ref[...] = ot.T.astype(o_ref.dtype)


def _attn_prompt(qt, kb, vt, lam_p, sub_g_col, lam_init, tq, tk):
    d, s = qt.shape
    nh = d // V_DIM
    return pl.pallas_call(
        functools.partial(_attn_prompt_kernel, lam_init=lam_init, tk=tk),
        grid=(nh, s // tq),
        in_specs=[
            pl.BlockSpec((V_DIM, tq), lambda h, i: (h, i)),
            pl.BlockSpec((s, V_DIM), lambda h, i: (0, h)),
            pl.BlockSpec((V_DIM, s), lambda h, i: (h, 0)),
            pl.BlockSpec(lam_p.shape, lambda h, i: (0, 0)),
            pl.BlockSpec((V_DIM, 1), lambda h, i: (0, 0)),
        ],
        out_specs=pl.BlockSpec((tq, V_DIM), lambda h, i: (i, h)),
        out_shape=jax.ShapeDtypeStruct((s, d), BF16),
        scratch_shapes=[pltpu.VMEM((tk, 2 * tq), F32), pltpu.VMEM((tk, 2 * tq), F32),
                        pltpu.VMEM((1, 2 * tq), F32), pltpu.VMEM((1, 2 * tq), F32),
                        pltpu.VMEM((V_DIM, 2 * tq), F32)],
        compiler_params=_cparams("parallel", "arbitrary"),
        name="attn_prompt",
    )(qt, kb, vt, lam_p, sub_g_col)


def _attn_sample_kernel(pt_ref, qk_ref, vn_ref, lam_ref, g_ref, e_ref, *refs,
                        lam_init, pages_per_step):
    k_refs = refs[:pages_per_step]
    v_refs = refs[pages_per_step:2 * pages_per_step]
    o_ref = refs[2 * pages_per_step]
    qb_ref, m_ref, l_ref, acc_ref = refs[2 * pages_per_step + 1:]
    step_idx = pl.program_id(1)
    n_maps, _, page = qb_ref.shape
    n_heads = n_maps // 2

    @pl.when(step_idx == 0)
    def _():
        qk = qk_ref[0]
        qt = qk.T
        for mp in range(n_maps):
            qb_ref[mp] = jnp.broadcast_to(qt[:HEAD_DIM, mp:mp + 1], (HEAD_DIM, page))
        m_ref[...] = jnp.sum(qk[:, :HEAD_DIM] * qk[:, HEAD_DIM:], axis=-1, keepdims=True)
        l_ref[...] = jnp.ones_like(l_ref)
        vn = vn_ref[0]
        for h in range(n_heads):
            acc_ref[2 * h:2 * h + 2, :] = jnp.broadcast_to(vn[h:h + 1, :], (2, V_DIM))

    slot = lax.broadcasted_iota(jnp.int32, (n_maps, page * n_heads), 1)
    mp_row = lax.broadcasted_iota(jnp.int32, (n_maps, page * n_heads), 0)
    own_head = (slot % n_heads) == (mp_row >> 1)
    for kp_ref, vp_ref in zip(k_refs, v_refs):
        s = jnp.sum(kp_ref[0] * qb_ref[...], axis=1)
        m_old = m_ref[...]
        m_new = jnp.maximum(m_old, jnp.max(s, axis=-1, keepdims=True))
        a = jnp.exp(m_old - m_new)
        p = jnp.exp(s - m_new)
        l_ref[...] = a * l_ref[...] + jnp.sum(p, axis=-1, keepdims=True)
        m_ref[...] = m_new
        pe = _dot(p.astype(BF16), e_ref[...])
        pm = jnp.where(own_head, pe, 0.0).astype(BF16)
        vb = vp_ref[0].reshape(page * n_heads, V_DIM).astype(BF16)
        acc_ref[...] = a * acc_ref[...] + _dot(pm, vb)

    @pl.when(step_idx == pl.num_programs(1) - 1)
    def _():
        acc_ref[...] = acc_ref[...] / l_ref[...]
        o1 = acc_ref[pl.ds(0, n_heads, stride=2), :]
        o2 = acc_ref[pl.ds(1, n_heads, stride=2), :]
        lam = _diff_lambda(lam_ref, lam_init)
        o_ref[0] = _diff_finalize(o1, o2, lam, g_ref[...], lam_init).astype(o_ref.dtype)


def _attn_sample(page_table, qk3, vn3, lam_p, sub_g, cache_kt, cache_v, lam_init, pages_per_step):
    nb, n_pages = page_table.shape
    _, n_maps, hd, page = cache_kt.shape
    n_heads = cache_v.shape[2]
    pps = pages_per_step

    def kspec(j):
        return pl.BlockSpec((1, n_maps, hd, page), lambda b, s, pt: (pt[b, s * pps + j], 0, 0, 0))

    def vspec(j):
        return pl.BlockSpec((1, page, n_heads, V_DIM), lambda b, s, pt: (pt[b, s * pps + j], 0, 0, 0))

    expand = (jnp.arange(page * n_heads)[None, :] // n_heads == jnp.arange(page)[:, None]).astype(BF16)
    grid_spec = pltpu.PrefetchScalarGridSpec(
        num_scalar_prefetch=1,
        grid=(nb, n_pages // pps),
        in_specs=[
            pl.BlockSpec((1, n_maps, 2 * hd), lambda b, s, pt: (b, 0, 0)),
            pl.BlockSpec((1, n_heads, V_DIM), lambda b, s, pt: (b, 0, 0)),
            pl.BlockSpec(lam_p.shape, lambda b, s, pt: (0, 0)),
            pl.BlockSpec((1, V_DIM), lambda b, s, pt: (0, 0)),
            pl.BlockSpec(expand.shape, lambda b, s, pt: (0, 0)),
        ] + [kspec(j) for j in range(pps)] + [vspec(j) for j in range(pps)],
        out_specs=pl.BlockSpec((1, n_heads, V_DIM), lambda b, s, pt: (b, 0, 0)),
        scratch_shapes=[
            pltpu.VMEM((n_maps, hd, page), F32),
            pltpu.VMEM((n_maps, 1), F32),
            pltpu.VMEM((n_maps, 1), F32),
            pltpu.VMEM((n_maps, V_DIM), F32),
        ],
    )
    return pl.pallas_call(
        functools.partial(_attn_sample_kernel, lam_init=lam_init, pages_per_step=pps),
        grid_spec=grid_spec,
        out_shape=jax.ShapeDtypeStruct((nb, n_heads, V_DIM), BF16),
        compiler_params=_cparams("parallel", "arbitrary"),
        name="attn_sample",
    )(page_table, qk3, vn3, lam_p, sub_g, expand, *([cache_kt] * pps), *([cache_v] * pps))


def _route(x, wr):
    n_exp = wr.shape[1]
    logits = jnp.dot(x, wr, preferred_element_type=F32, precision=lax.Precision.HIGHEST)
    idx = lax.broadcasted_iota(jnp.int32, logits.shape, 1)
    v1 = jnp.max(logits, axis=-1, keepdims=True)
    i1 = jnp.min(jnp.where(logits == v1, idx, n_exp), axis=-1, keepdims=True)
    rest = jnp.where(idx == i1, -jnp.inf, logits)
    v2 = jnp.max(rest, axis=-1, keepdims=True)
    i2 = jnp.min(jnp.where(rest == v2, idx, n_exp), axis=-1, keepdims=True)
    t = jnp.exp(v2 - v1)
    g1 = 1.0 / (1.0 + t)
    g2 = t / (1.0 + t)
    return jnp.where(idx == i1, g1, 0.0) + jnp.where(idx == i2, g2, 0.0)


def _wo_kernel(a_ref, x_ref, wo_ref, pg_ref, pb_ref, wr_ref, o_ref, xb_ref, comb_ref, *, alpha):
    m = _dot(a_ref[...], wo_ref[...])
    x1 = _ln(alpha * x_ref[...] + m, pg_ref[...], pb_ref[...])
    o_ref[...] = x1
    xb_ref[...] = x1.astype(BF16)
    comb_ref[...] = _route(x1, wr_ref[...])


def _wo(a, x, wo, pg, pb, wr, alpha, tm):
    m, d = x.shape
    n_exp = wr.shape[1]
    vec = pl.BlockSpec((1, d), lambda i: (0, 0))
    row = pl.BlockSpec((tm, d), lambda i: (i, 0))
    return pl.pallas_call(
        functools.partial(_wo_kernel, alpha=alpha),
        grid=(m // tm,),
        in_specs=[row, row, pl.BlockSpec((d, d), lambda i: (0, 0)), vec, vec,
                  pl.BlockSpec((d, n_exp), lambda i: (0, 0))],
        out_specs=[row, row, pl.BlockSpec((tm, n_exp), lambda i: (i, 0))],
        out_shape=[jax.ShapeDtypeStruct((m, d), F32), jax.ShapeDtypeStruct((m, d), BF16),
                   jax.ShapeDtypeStruct((m, n_exp), F32)],
        compiler_params=_cparams("parallel"),
        name="wo",
    )(a, x, wo, pg, pb, wr)


def _moe_kernel(xb_ref, comb_ref, wg_ref, wu_ref, wd_ref, o_ref, *, sub):
    e = pl.program_id(1)
    tm = xb_ref.shape[0]
    comb = comb_ref[...]
    idx = lax.broadcasted_iota(jnp.int32, comb.shape, 1)
    w = jnp.sum(jnp.where(idx == e, comb, 0.0), axis=-1, keepdims=True)

    @pl.when(e == 0)
    def _():
        o_ref[...] = jnp.zeros(o_ref.shape, F32)

    for r0 in range(0, tm, sub):
        rows = slice(r0, r0 + sub)
        xb = xb_ref[rows, :]
        h = _silu(_dot(xb, wg_ref[0])) * _dot(xb, wu_ref[0])
        o_ref[rows, :] += _dot((h * w[rows, :]).astype(BF16), wd_ref[0])


def _moe(xb, comb, wg, wu, wd, tm):
    m, d = xb.shape
    n_exp, _, dexp = wg.shape
    return pl.pallas_call(
        functools.partial(_moe_kernel, sub=min(tm, MOE_SUB_ROWS)),
        grid=(m // tm, n_exp),
        in_specs=[
            pl.BlockSpec((tm, d), lambda i, e: (i, 0)),
            pl.BlockSpec((tm, n_exp), lambda i, e: (i, 0)),
            pl.BlockSpec((1, d, dexp), lambda i, e: (e, 0, 0)),
            pl.BlockSpec((1, d, dexp), lambda i, e: (e, 0, 0)),
            pl.BlockSpec((1, dexp, d), lambda i, e: (e, 0, 0)),
        ],
        out_specs=pl.BlockSpec((tm, d), lambda i, e: (i, 0)),
        out_shape=jax.ShapeDtypeStruct((m, d), F32),
        compiler_params=_cparams("parallel", "arbitrary"),
        name="moe",
    )(xb, comb, wg, wu, wd)


def kernel(x_prompt, x_sample, state_conv, cache_k, cache_v, page_table, p_prompt, p_sample,
           conv_w_pw1, conv_b_pw1, conv_w_dw, conv_b_dw, conv_ln_g, conv_ln_b, conv_w_pw2,
           kv_w_k, kv_w_v, attn_w_q, attn_lambda, attn_sub_g, attn_w_o,
           ffn_w_gate, ffn_w_up, ffn_w_down, moe_w_router, moe_w_gate, moe_w_up, moe_w_down,
           post_ln_g, post_ln_b, ple_w_proj, ple_w_gate):
    bp, sp, d = x_prompt.shape
    bd, sd, _ = x_sample.shape
    depth = post_ln_g.shape[0]
    n_a = conv_w_pw1.shape[0]
    assert bp == 1 and sd == 1 and depth == 2 and n_a == 1
    n_heads = d // V_DIM
    n_maps = 2 * n_heads
    past = page_table.shape[1] * cache_k.shape[1]
    alpha = (2.0 * depth) ** 0.25
    lam_init = 0.8 - 0.6 * math.exp(-0.3 * 1)

    bf = lambda w: w.astype(BF16)
    row = lambda v: v.reshape(1, -1)

    xp = x_prompt.reshape(sp, d)
    xd = x_sample.reshape(bd, d)
    pp = p_prompt.reshape(depth, sp, -1)
    pdm = p_sample.reshape(depth, bd, -1)

    w1, w2 = bf(conv_w_pw1[0]), bf(conv_w_pw2[0])
    b1 = row(conv_b_pw1[0])
    conv_args = (conv_w_dw[0], row(conv_b_dw[0]), row(conv_ln_g[0]), row(conv_ln_b[0]), w2,
                 row(post_ln_g[0, 0]), row(post_ln_b[0, 0]), alpha)
    up = _glu(xp, w1, b1, 512, 512)
    ud = _glu(xd, w1, b1, bd, 512)
    conv_prompt = up[sp - (CONV_W - 1):].reshape(1, 1, CONV_W - 1, d)
    hist = jnp.transpose(state_conv[0], (1, 0, 2))
    conv_sample = jnp.transpose(jnp.concatenate([hist[1:], ud[None]], axis=0), (1, 0, 2))[None]
    xp = _conv_prompt(up, xp, *conv_args, 256)
    xd = _conv_sample(hist, ud, xd, *conv_args)

    fw = (bf(ffn_w_gate[0]), bf(ffn_w_up[0]), bf(ffn_w_down[0]), row(post_ln_g[0, 1]), row(post_ln_b[0, 1]), alpha)
    xp = _swiglu(xp, *fw, 512, 512)
    xd = _swiglu(xd, *fw, bd, 512)

    pw = (bf(ple_w_gate[0]), bf(ple_w_proj[0]))
    xp = _ple(xp, pp[0], *pw, 1024, 512)
    xd = _ple(xd, pdm[0], *pw, bd, 512)

    wk, wv, wq = bf(kv_w_k), bf(kv_w_v), bf(attn_w_q[0])
    cos_p, sin_p = _rope_tables(jnp.arange(sp))
    cos_d, sin_d = _rope_tables(jnp.full((bd,), past))
    k_p, v_p, kb_p, qt_p, vt_p = _proj(xp, wk, wv, wq, cos_p, sin_p, 512, 512, True)
    k_d, v_d, qb_d = _proj(xd, wk, wv, wq, cos_d, sin_d, bd, 512, False)

    lam_p = attn_lambda[0]
    sub_g = row(attn_sub_g[0])
    ap = _attn_prompt(qt_p, kb_p, vt_p, lam_p, attn_sub_g[0].reshape(V_DIM, 1), lam_init, 512, 512)
    cache_kt = jnp.transpose(cache_k, (0, 2, 3, 1))
    qk3 = jnp.concatenate([qb_d.astype(F32).reshape(bd, n_maps, HEAD_DIM),
                           k_d.astype(BF16).astype(F32).reshape(bd, n_maps, HEAD_DIM)], axis=-1)
    ad = _attn_sample(page_table, qk3, v_d.reshape(bd, n_heads, V_DIM), lam_p, sub_g,
                      cache_kt, cache_v, lam_init, 8).reshape(bd, d)
    wo_args = (bf(attn_w_o[0]), row(post_ln_g[1, 0]), row(post_ln_b[1, 0]), moe_w_router[0], alpha)
    xp, xbp, combp = _wo(ap, xp, *wo_args, 256)
    xd, xbd, combd = _wo(ad, xd, *wo_args, bd)

    mw = (bf(moe_w_gate[0]), bf(moe_w_up[0]), bf(moe_w_down[0]))
    fp = _moe(xbp, combp, *mw, 512)
    fd = _moe(xbd, combd, *mw, bd)

    pw = (bf(ple_w_gate[1]), bf(ple_w_proj[1]))
    post = (row(post_ln_g[1, 1]), row(post_ln_b[1, 1]), alpha)
    xp = _ple(xp, pp[1], *pw, 512, 512, post=(fp,) + post)
    xd = _ple(xd, pdm[1], *pw, bd, 512, post=(fd,) + post)

    return (xp.reshape(bp, sp, d), xd.reshape(bd, sd, d), conv_prompt, conv_sample,
            k_p.reshape(bp, sp, n_maps, HEAD_DIM), v_p.reshape(bp, sp, n_heads, V_DIM),
            k_d.reshape(bd, sd, n_maps, HEAD_DIM), v_d.reshape(bd, sd, n_heads, V_DIM))
```

```python
import functools
import math

import jax
import jax.numpy as jnp
from jax import lax
from jax.experimental import pallas as pl
from jax.experimental.pallas import tpu as pltpu

HEAD_DIM = 64
V_DIM = 2 * HEAD_DIM
CONV_W = 31
TOP_K = 2
ROPE_THETA = 10000.0
LN_EPS = 1e-5
ATTN_SCALE = HEAD_DIM ** -0.5
LOG2E = math.log2(math.e)

LANES = 128
SUBLANES = 8
MOE_SUB_ROWS = 256
HALO = 32
VMEM_LIMIT = 56 * 1024 * 1024

F32 = jnp.float32
BF16 = jnp.bfloat16


def _cparams(*sem):
    return pltpu.CompilerParams(dimension_semantics=sem, vmem_limit_bytes=VMEM_LIMIT)


def _dot(a, b):
    return jnp.dot(a, b, preferred_element_type=F32)


def _ln(x, g, b):
    mu = jnp.mean(x, axis=-1, keepdims=True)
    xc = x - mu
    var = jnp.mean(xc * xc, axis=-1, keepdims=True)
    return xc * lax.rsqrt(var + LN_EPS) * g + b


def _sigmoid(x):
    return 1.0 / (1.0 + jnp.exp(-x))


def _silu(x):
    return x * _sigmoid(x)


def _glu_kernel(x_ref, wa_ref, wg_ref, ba_ref, bg_ref, u_ref, xb_ref):
    @pl.when(pl.program_id(1) == 0)
    def _():
        xb_ref[...] = x_ref[...].astype(BF16)

    xb = xb_ref[...]
    a = _dot(xb, wa_ref[...]) + ba_ref[...]
    g = _dot(xb, wg_ref[...]) + bg_ref[...]
    u_ref[...] = a * _sigmoid(g)


def _glu(x, w1, b1, tm, tn):
    m, d = x.shape
    nj = d // tn
    return pl.pallas_call(
        _glu_kernel,
        grid=(m // tm, nj),
        in_specs=[
            pl.BlockSpec((tm, d), lambda i, j: (i, 0)),
            pl.BlockSpec((d, tn), lambda i, j: (0, j)),
            pl.BlockSpec((d, tn), lambda i, j: (0, j + nj)),
            pl.BlockSpec((1, tn), lambda i, j: (0, j)),
            pl.BlockSpec((1, tn), lambda i, j: (0, j + nj)),
        ],
        out_specs=pl.BlockSpec((tm, tn), lambda i, j: (i, j)),
        out_shape=jax.ShapeDtypeStruct((m, d), F32),
        scratch_shapes=[pltpu.VMEM((tm, d), BF16)],
        compiler_params=_cparams("parallel", "arbitrary"),
        name="glu",
    )(x, w1, w1, b1, b1)


def _mix_tail(y, x, g_ref, b_ref, w2_ref, pg_ref, pb_ref, alpha):
    y = _silu(_ln(y, g_ref[...], b_ref[...]))
    m = _dot(y.astype(BF16), w2_ref[...])
    return _ln(alpha * x + m, pg_ref[...], pb_ref[...])


def _conv_prompt_kernel(uprev_ref, u_ref, x_ref, wdw_ref, bdw_ref, g_ref, b_ref, w2_ref,
                        pg_ref, pb_ref, o_ref, full_ref, y_ref, *, alpha, rows):
    tm, d = u_ref.shape
    first = pl.program_id(0) == 0
    full_ref[0:HALO, :] = jnp.where(first, 0.0, uprev_ref[...])
    full_ref[HALO:, :] = u_ref[...]
    off = HALO - (CONV_W - 1)

    def col_body(c, carry):
        cs = pl.ds(pl.multiple_of(c * LANES, LANES), LANES)
        w = wdw_ref[:, cs]
        bias = bdw_ref[:, cs]

        def row_body(r, carry2):
            base = pl.multiple_of(r * rows, rows)
            win = full_ref[pl.ds(base, rows + HALO), cs]
            acc = jnp.zeros((rows, LANES), F32)
            for res in range(SUBLANES):
                sh = win if res == 0 else pltpu.roll(win, rows + HALO - res, 0)
                for t in range(CONV_W):
                    if (off + t) % SUBLANES == res:
                        a8 = off + t - res
                        acc = acc + sh[a8:a8 + rows, :] * w[t:t + 1, :]
            y_ref[pl.ds(base, rows), cs] = acc + bias
            return carry2

        return lax.fori_loop(0, tm // rows, row_body, carry)

    lax.fori_loop(0, d // LANES, col_body, 0)
    o_ref[...] = _mix_tail(y_ref[...], x_ref[...], g_ref, b_ref, w2_ref, pg_ref, pb_ref, alpha)


def _conv_prompt(u, x, wdw, bdw, g, b, w2, pg, pb, alpha, tm):
    m, d = x.shape
    nh = tm // HALO
    vec = pl.BlockSpec((1, d), lambda i: (0, 0))
    return pl.pallas_call(
        functools.partial(_conv_prompt_kernel, alpha=alpha, rows=64),
        grid=(m // tm,),
        in_specs=[
            pl.BlockSpec((HALO, d), lambda i: (jnp.maximum(i * nh - 1, 0), 0)),
            pl.BlockSpec((tm, d), lambda i: (i, 0)),
            pl.BlockSpec((tm, d), lambda i: (i, 0)),
            pl.BlockSpec((CONV_W, d), lambda i: (0, 0)),
            vec, vec, vec,
            pl.BlockSpec((d, d), lambda i: (0, 0)),
            vec, vec,
        ],
        out_specs=pl.BlockSpec((tm, d), lambda i: (i, 0)),
        out_shape=jax.ShapeDtypeStruct((m, d), F32),
        scratch_shapes=[pltpu.VMEM((tm + HALO, d), F32), pltpu.VMEM((tm, d), F32)],
        compiler_params=_cparams("parallel"),
        name="conv_prompt",
    )(u, u, x, wdw, bdw, g, b, w2, pg, pb)


def _conv_sample_kernel(hist_ref, u_ref, x_ref, wdw_ref, bdw_ref, g_ref, b_ref, w2_ref,
                        pg_ref, pb_ref, o_ref, *, alpha):
    y = u_ref[...] * wdw_ref[CONV_W - 1:CONV_W, :]
    for t in range(CONV_W - 1):
        y = y + hist_ref[t] * wdw_ref[t:t + 1, :]
    y = y + bdw_ref[...]
    o_ref[...] = _mix_tail(y, x_ref[...], g_ref, b_ref, w2_ref, pg_ref, pb_ref, alpha)


def _conv_sample(hist, u, x, wdw, bdw, g, b, w2, pg, pb, alpha):
    m, d = x.shape
    return pl.pallas_call(
        functools.partial(_conv_sample_kernel, alpha=alpha),
        out_shape=jax.ShapeDtypeStruct((m, d), F32),
        compiler_params=pltpu.CompilerParams(vmem_limit_bytes=VMEM_LIMIT),
        name="conv_sample",
    )(hist, u, x, wdw, bdw, g, b, w2, pg, pb)


def _swiglu_kernel(x_ref, xs_ref, wg_ref, wu_ref, wd_ref, pg_ref, pb_ref, o_ref, os_ref, xb_ref, xsb_ref,
                   *, alpha):
    f = pl.program_id(1)

    def tile(xin_ref, out_ref, xbf_ref):
        @pl.when(f == 0)
        def _():
            xbf_ref[...] = xin_ref[...].astype(BF16)

        xb = xbf_ref[...]
        h = _silu(_dot(xb, wg_ref[...])) * _dot(xb, wu_ref[...])
        part = _dot(h.astype(BF16), wd_ref[...])

        @pl.when(f == 0)
        def _():
            out_ref[...] = part

        @pl.when(f > 0)
        def _():
            out_ref[...] += part

        @pl.when(f == pl.num_programs(1) - 1)
        def _():
            out_ref[...] = _ln(alpha * xin_ref[...] + out_ref[...], pg_ref[...], pb_ref[...])

    tile(x_ref, o_ref, xb_ref)

    @pl.when(pl.program_id(0) == 0)
    def _():
        tile(xs_ref, os_ref, xsb_ref)


def _swiglu(x, xs, wg, wu, wd, pg, pb, alpha, tm, tf):
    m, d = x.shape
    ms = xs.shape[0]
    dff = wg.shape[1]
    vec = pl.BlockSpec((1, d), lambda i, f: (0, 0))
    small = pl.BlockSpec((ms, d), lambda i, f: (0, 0))
    return pl.pallas_call(
        functools.partial(_swiglu_kernel, alpha=alpha),
        grid=(m // tm, dff // tf),
        in_specs=[
            pl.BlockSpec((tm, d), lambda i, f: (i, 0)),
            small,
            pl.BlockSpec((d, tf), lambda i, f: (0, f)),
            pl.BlockSpec((d, tf), lambda i, f: (0, f)),
            pl.BlockSpec((tf, d), lambda i, f: (f, 0)),
            vec, vec,
        ],
        out_specs=[pl.BlockSpec((tm, d), lambda i, f: (i, 0)), small],
        out_shape=[jax.ShapeDtypeStruct((m, d), F32), jax.ShapeDtypeStruct((ms, d), F32)],
        scratch_shapes=[pltpu.VMEM((tm, d), BF16), pltpu.VMEM((ms, d), BF16)],
        compiler_params=_cparams("arbitrary", "arbitrary"),
        name="swiglu",
    )(x, xs, wg, wu, wd, pg, pb)


def _ple_kernel(*refs, alpha, post_ln):
    if post_ln:
        x_ref, f_ref, pg_ref, pb_ref, p_ref, wg_ref, wp_ref, o_ref, xb_ref, xf_ref = refs
    else:
        x_ref, p_ref, wg_ref, wp_ref, o_ref, xb_ref = refs
        xf_ref = x_ref
    j = pl.program_id(1)
    tn = o_ref.shape[1]

    @pl.when(j == 0)
    def _():
        if post_ln:
            xf_ref[...] = _ln(alpha * x_ref[...] + f_ref[...], pg_ref[...], pb_ref[...])
        xb_ref[...] = xf_ref[...].astype(BF16)

    gate = _sigmoid(_dot(xb_ref[...], wg_ref[...]))
    xc = xf_ref[:, pl.ds(pl.multiple_of(j * tn, tn), tn)]
    o_ref[...] = xc + gate * _dot(p_ref[...].astype(BF16), wp_ref[...])


def _ple(x, p, wg, wp, tm, tn, post=None):
    m, d = x.shape
    pd = p.shape[1]
    row = pl.BlockSpec((tm, d), lambda i, j: (i, 0))
    vec = pl.BlockSpec((1, d), lambda i, j: (0, 0))
    tail_specs = [pl.BlockSpec((tm, pd), lambda i, j: (i, 0)),
                  pl.BlockSpec((d, tn), lambda i, j: (0, j)),
                  pl.BlockSpec((pd, tn), lambda i, j: (0, j))]
    scratch = [pltpu.VMEM((tm, d), BF16)]
    if post is None:
        args, in_specs, alpha = (x, p, wg, wp), [row] + tail_specs, 1.0
    else:
        f, pg, pb, alpha = post
        args, in_specs = (x, f, pg, pb, p, wg, wp), [row, row, vec, vec] + tail_specs
        scratch.append(pltpu.VMEM((tm, d), F32))
    return pl.pallas_call(
        functools.partial(_ple_kernel, alpha=alpha, post_ln=post is not None),
        grid=(m // tm, d // tn),
        in_specs=in_specs,
        out_specs=pl.BlockSpec((tm, tn), lambda i, j: (i, j)),
        out_shape=jax.ShapeDtypeStruct((m, d), F32),
        scratch_shapes=scratch,
        compiler_params=_cparams("parallel", "arbitrary"),
        name="ple",
    )(*args)


def _rope_tables(pos):
    half = HEAD_DIM // 2
    inv = ROPE_THETA ** (-jnp.arange(half, dtype=F32) / half)
    ang = pos.astype(F32)[:, None] * inv[None, :]
    cos, sin = jnp.cos(ang), jnp.sin(ang)
    cos_t = jnp.concatenate([cos, cos, cos, cos], axis=-1)
    sin_t = jnp.concatenate([-sin, sin, -sin, sin], axis=-1)
    return cos_t, sin_t


def _rope_tile(h, cos, sin, first_half):
    partner = jnp.where(first_half, pltpu.roll(h, LANES - HEAD_DIM // 2, 1),
                        pltpu.roll(h, HEAD_DIM // 2, 1))
    return h * cos + partner * sin


def _proj_kernel(x_ref, wk_ref, wv_ref, wq_ref, cos_ref, sin_ref, *refs, transposed):
    if transposed:
        k_ref, v_ref, kb_ref, qt_ref, vt_ref, xb_ref = refs
    else:
        k_ref, v_ref, qb_ref, xb_ref = refs

    @pl.when(pl.program_id(1) == 0)
    def _():
        xb_ref[...] = x_ref[...].astype(BF16)

    xb = xb_ref[...]
    tm, tn = k_ref.shape
    cos, sin = cos_ref[...], sin_ref[...]
    lane = lax.broadcasted_iota(jnp.int32, (tm, LANES), 1)
    first_half = (lane % HEAD_DIM) < (HEAD_DIM // 2)
    hk = _dot(xb, wk_ref[...])
    hq = _dot(xb, wq_ref[...])
    for c in range(tn // LANES):
        cs = slice(c * LANES, (c + 1) * LANES)
        kr = _rope_tile(hk[:, cs], cos, sin, first_half)
        k_ref[:, cs] = kr
        qr = _rope_tile(hq[:, cs], cos, sin, first_half)
        if transposed:
            kb_ref[:, cs] = kr.astype(BF16)
            qt_ref[cs, :] = (qr * (ATTN_SCALE * LOG2E)).T.astype(BF16)
        else:
            qb_ref[:, cs] = (qr * ATTN_SCALE).astype(BF16)
    hv = _dot(xb, wv_ref[...])
    v_ref[...] = hv
    if transposed:
        vt_ref[...] = hv.T.astype(BF16)


def _proj(x, wk, wv, wq, cos_t, sin_t, tm, tn, transposed):
    m, d = x.shape
    wspec = pl.BlockSpec((d, tn), lambda i, j: (0, j))
    tspec = pl.BlockSpec((tm, LANES), lambda i, j: (i, 0))
    ospec = pl.BlockSpec((tm, tn), lambda i, j: (i, j))
    tr_spec = pl.BlockSpec((tn, tm), lambda i, j: (j, i))
    f32_out = jax.ShapeDtypeStruct((m, d), F32)
    if transposed:
        out_specs = [ospec, ospec, ospec, tr_spec, tr_spec]
        out_shape = [f32_out, f32_out, jax.ShapeDtypeStruct((m, d), BF16),
                     jax.ShapeDtypeStruct((d, m), BF16), jax.ShapeDtypeStruct((d, m), BF16)]
    else:
        out_specs = [ospec, ospec, ospec]
        out_shape = [f32_out, f32_out, jax.ShapeDtypeStruct((m, d), BF16)]
    return pl.pallas_call(
        functools.partial(_proj_kernel, transposed=transposed),
        grid=(m // tm, d // tn),
        in_specs=[pl.BlockSpec((tm, d), lambda i, j: (i, 0)), wspec, wspec, wspec, tspec, tspec],
        out_specs=out_specs,
        out_shape=out_shape,
        scratch_shapes=[pltpu.VMEM((tm, d), BF16)],
        compiler_params=_cparams("parallel", "arbitrary"),
        name="proj",
    )(x, wk, wv, wq, cos_t, sin_t)


def _diff_lambda(lam_ref, lam_init):
    lp = lam_ref[...]
    s1 = jnp.sum(lp[0:1, :] * lp[1:2, :], axis=-1, keepdims=True)
    s2 = jnp.sum(lp[2:3, :] * lp[3:4, :], axis=-1, keepdims=True)
    return jnp.exp(s1) - jnp.exp(s2) + lam_init


def _diff_finalize(o1, o2, lam, g, lam_init):
    o = o1 - lam * o2
    o = o * lax.rsqrt(jnp.mean(o * o, axis=-1, keepdims=True) + LN_EPS)
    return o * g * (1.0 - lam_init)


def _attn_prompt_kernel(qt_ref, k_ref, vt_ref, lam_ref, g_ref, o_ref,
                        sa_ref, sb_ref, m_ref, l_ref, acc_ref, *, lam_init, tk):
    tq = qt_ref.shape[1]
    i = pl.program_id(1)
    q0 = i * tq
    n_full = q0 // tk
    qt = qt_ref[...]
    feat = lax.broadcasted_iota(jnp.int32, qt.shape, 0)
    qz = jnp.concatenate([jnp.where(feat < HEAD_DIM, qt, 0), jnp.where(feat >= HEAD_DIM, qt, 0)], axis=1)
    m_ref[...] = jnp.full(m_ref.shape, -jnp.inf, F32)
    l_ref[...] = jnp.zeros(l_ref.shape, F32)
    acc_ref[...] = jnp.zeros(acc_ref.shape, F32)

    def scores(b, dst_ref):
        base = pl.multiple_of(b * tk, tk)
        dst_ref[...] = _dot(k_ref[pl.ds(base, tk), :], qz)

    def softmax_pv(b, src_ref, masked):
        base = pl.multiple_of(b * tk, tk)
        s = src_ref[...]
        if masked:
            key = base + lax.broadcasted_iota(jnp.int32, s.shape, 0)
            qry = q0 + lax.broadcasted_iota(jnp.int32, s.shape, 1) % tq
            s = jnp.where(key <= qry, s, -jnp.inf)
        m_old = m_ref[...]
        m_new = jnp.maximum(m_old, jnp.max(s, axis=0, keepdims=True))
        a = jnp.exp2(m_old - m_new)
        p = jnp.exp2(s - m_new)
        l_ref[...] = a * l_ref[...] + jnp.sum(p, axis=0, keepdims=True)
        m_ref[...] = m_new
        acc_ref[...] = a * acc_ref[...] + _dot(vt_ref[:, pl.ds(base, tk)], p.astype(BF16))

    scores(0, sa_ref)

    def pair(t, carry):
        b = 2 * t
        scores(b + 1, sb_ref)
        softmax_pv(b, sa_ref, False)
        scores(b + 2, sa_ref)
        softmax_pv(b + 1, sb_ref, False)
        return carry

    lax.fori_loop(0, n_full // 2, pair, 0)

    @pl.when(n_full % 2 == 1)
    def _():
        scores(n_full, sb_ref)
        softmax_pv(n_full - 1, sa_ref, False)
        softmax_pv(n_full, sb_ref, True)

    @pl.when(n_full % 2 == 0)
    def _():
        softmax_pv(n_full, sa_ref, True)

    lam = _diff_lambda(lam_ref, lam_init)
    o = acc_ref[...] * (1.0 / l_ref[...])
    ot = o[:, :tq] - lam * o[:, tq:]
    ot = ot * lax.rsqrt(jnp.mean(ot * ot, axis=0, keepdims=True) + LN_EPS)
    ot = ot * g_ref[...] * (1.0 - lam_init)
    o_ref[...] = ot.T.astype(o_ref.dtype)


def _attn_prompt(qt, kb, vt, lam_p, sub_g_col, lam_init, tq, tk):
    d, s = qt.shape
    nh = d // V_DIM
    return pl.pallas_call(
        functools.partial(_attn_prompt_kernel, lam_init=lam_init, tk=tk),
        grid=(nh, s // tq),
        in_specs=[
            pl.BlockSpec((V_DIM, tq), lambda h, i: (h, i)),
            pl.BlockSpec((s, V_DIM), lambda h, i: (0, h)),
            pl.BlockSpec((V_DIM, s), lambda h, i: (h, 0)),
            pl.BlockSpec(lam_p.shape, lambda h, i: (0, 0)),
            pl.BlockSpec((V_DIM, 1), lambda h, i: (0, 0)),
        ],
        out_specs=pl.BlockSpec((tq, V_DIM), lambda h, i: (i, h)),
        out_shape=jax.ShapeDtypeStruct((s, d), BF16),
        scratch_shapes=[pltpu.VMEM((tk, 2 * tq), F32), pltpu.VMEM((tk, 2 * tq), F32),
                        pltpu.VMEM((1, 2 * tq), F32), pltpu.VMEM((1, 2 * tq), F32),
                        pltpu.VMEM((V_DIM, 2 * tq), F32)],
        compiler_params=_cparams("parallel", "arbitrary"),
        name="attn_prompt",
    )(qt, kb, vt, lam_p, sub_g_col)


def _attn_sample_kernel(pt_ref, qk_ref, vn_ref, lam_ref, g_ref, e_ref, *refs,
                        lam_init, pages_per_step):
    k_refs = refs[:pages_per_step]
    v_refs = refs[pages_per_step:2 * pages_per_step]
    o_ref = refs[2 * pages_per_step]
    qb_ref, m_ref, l_ref, acc_ref = refs[2 * pages_per_step + 1:]
    step_idx = pl.program_id(1)
    n_maps, _, page = qb_ref.shape
    n_heads = n_maps // 2

    @pl.when(step_idx == 0)
    def _():
        qk = qk_ref[0]
        qt = qk.T
        for mp in range(n_maps):
            qb_ref[mp] = jnp.broadcast_to(qt[:HEAD_DIM, mp:mp + 1], (HEAD_DIM, page))
        m_ref[...] = jnp.sum(qk[:, :HEAD_DIM] * qk[:, HEAD_DIM:], axis=-1, keepdims=True)
        l_ref[...] = jnp.ones_like(l_ref)
        vn = vn_ref[0]
        for h in range(n_heads):
            acc_ref[2 * h:2 * h + 2, :] = jnp.broadcast_to(vn[h:h + 1, :], (2, V_DIM))

    slot = lax.broadcasted_iota(jnp.int32, (n_maps, page * n_heads), 1)
    mp_row = lax.broadcasted_iota(jnp.int32, (n_maps, page * n_heads), 0)
    own_head = (slot % n_heads) == (mp_row >> 1)
    for kp_ref, vp_ref in zip(k_refs, v_refs):
        s = jnp.sum(kp_ref[0] * qb_ref[...], axis=1)
        m_old = m_ref[...]
        m_new = jnp.maximum(m_old, jnp.max(s, axis=-1, keepdims=True))
        a = jnp.exp(m_old - m_new)
        p = jnp.exp(s - m_new)
        l_ref[...] = a * l_ref[...] + jnp.sum(p, axis=-1, keepdims=True)
        m_ref[...] = m_new
        pe = _dot(p.astype(BF16), e_ref[...])
        pm = jnp.where(own_head, pe, 0.0).astype(BF16)
        vb = vp_ref[0].reshape(page * n_heads, V_DIM).astype(BF16)
        acc_ref[...] = a * acc_ref[...] + _dot(pm, vb)

    @pl.when(step_idx == pl.num_programs(1) - 1)
    def _():
        acc_ref[...] = acc_ref[...] / l_ref[...]
        o1 = acc_ref[pl.ds(0, n_heads, stride=2), :]
        o2 = acc_ref[pl.ds(1, n_heads, stride=2), :]
        lam = _diff_lambda(lam_ref, lam_init)
        o_ref[0] = _diff_finalize(o1, o2, lam, g_ref[...], lam_init).astype(o_ref.dtype)


def _attn_sample(page_table, qk3, vn3, lam_p, sub_g, cache_kt, cache_v, lam_init, pages_per_step):
    nb, n_pages = page_table.shape
    _, n_maps, hd, page = cache_kt.shape
    n_heads = cache_v.shape[2]
    pps = pages_per_step

    def kspec(j):
        return pl.BlockSpec((1, n_maps, hd, page), lambda b, s, pt: (pt[b, s * pps + j], 0, 0, 0))

    def vspec(j):
        return pl.BlockSpec((1, page, n_heads, V_DIM), lambda b, s, pt: (pt[b, s * pps + j], 0, 0, 0))

    expand = (jnp.arange(page * n_heads)[None, :] // n_heads == jnp.arange(page)[:, None]).astype(BF16)
    grid_spec = pltpu.PrefetchScalarGridSpec(
        num_scalar_prefetch=1,
        grid=(nb, n_pages // pps),
        in_specs=[
            pl.BlockSpec((1, n_maps, 2 * hd), lambda b, s, pt: (b, 0, 0)),
            pl.BlockSpec((1, n_heads, V_DIM), lambda b, s, pt: (b, 0, 0)),
            pl.BlockSpec(lam_p.shape, lambda b, s, pt: (0, 0)),
            pl.BlockSpec((1, V_DIM), lambda b, s, pt: (0, 0)),
            pl.BlockSpec(expand.shape, lambda b, s, pt: (0, 0)),
        ] + [kspec(j) for j in range(pps)] + [vspec(j) for j in range(pps)],
        out_specs=pl.BlockSpec((1, n_heads, V_DIM), lambda b, s, pt: (b, 0, 0)),
        scratch_shapes=[
            pltpu.VMEM((n_maps, hd, page), F32),
            pltpu.VMEM((n_maps, 1), F32),
            pltpu.VMEM((n_maps, 1), F32),
            pltpu.VMEM((n_maps, V_DIM), F32),
        ],
    )
    return pl.pallas_call(
        functools.partial(_attn_sample_kernel, lam_init=lam_init, pages_per_step=pps),
        grid_spec=grid_spec,
        out_shape=jax.ShapeDtypeStruct((nb, n_heads, V_DIM), BF16),
        compiler_params=_cparams("parallel", "arbitrary"),
        name="attn_sample",
    )(page_table, qk3, vn3, lam_p, sub_g, expand, *([cache_kt] * pps), *([cache_v] * pps))


def _route(xb, wr):
    n_exp = wr.shape[1]
    logits = _dot(xb, wr.astype(BF16))
    idx = lax.broadcasted_iota(jnp.int32, logits.shape, 1)
    v1 = jnp.max(logits, axis=-1, keepdims=True)
    i1 = jnp.min(jnp.where(logits == v1, idx, n_exp), axis=-1, keepdims=True)
    rest = jnp.where(idx == i1, -jnp.inf, logits)
    v2 = jnp.max(rest, axis=-1, keepdims=True)
    i2 = jnp.min(jnp.where(rest == v2, idx, n_exp), axis=-1, keepdims=True)
    t = jnp.exp(v2 - v1)
    g1 = 1.0 / (1.0 + t)
    g2 = t / (1.0 + t)
    return jnp.where(idx == i1, g1, 0.0) + jnp.where(idx == i2, g2, 0.0)


def _wo_kernel(a_ref, x_ref, wo_ref, pg_ref, pb_ref, wr_ref, o_ref, xb_ref, comb_ref, *, alpha):
    m = _dot(a_ref[...], wo_ref[...])
    x1 = _ln(alpha * x_ref[...] + m, pg_ref[...], pb_ref[...])
    o_ref[...] = x1
    xb = x1.astype(BF16)
    xb_ref[...] = xb
    comb_ref[...] = _route(xb, wr_ref[...])


def _wo(a, x, wo, pg, pb, wr, alpha, tm):
    m, d = x.shape
    n_exp = wr.shape[1]
    vec = pl.BlockSpec((1, d), lambda i: (0, 0))
    row = pl.BlockSpec((tm, d), lambda i: (i, 0))
    return pl.pallas_call(
        functools.partial(_wo_kernel, alpha=alpha),
        grid=(m // tm,),
        in_specs=[row, row, pl.BlockSpec((d, d), lambda i: (0, 0)), vec, vec,
                  pl.BlockSpec((d, n_exp), lambda i: (0, 0))],
        out_specs=[row, row, pl.BlockSpec((tm, n_exp), lambda i: (i, 0))],
        out_shape=[jax.ShapeDtypeStruct((m, d), F32), jax.ShapeDtypeStruct((m, d), BF16),
                   jax.ShapeDtypeStruct((m, n_exp), F32)],
        compiler_params=_cparams("parallel"),
        name="wo",
    )(a, x, wo, pg, pb, wr)


def _moe_kernel(xb_ref, comb_ref, xsb_ref, combs_ref, wg_ref, wu_ref, wd_ref, o_ref, os_ref, *, sub):
    e = pl.program_id(1)

    def tile(xin_ref, cin_ref, out_ref, step):
        comb = cin_ref[...]
        idx = lax.broadcasted_iota(jnp.int32, comb.shape, 1)
        w = jnp.sum(jnp.where(idx == e, comb, 0.0), axis=-1, keepdims=True)

        @pl.when(e == 0)
        def _():
            out_ref[...] = jnp.zeros(out_ref.shape, F32)

        for r0 in range(0, xin_ref.shape[0], step):
            rows = slice(r0, r0 + step)
            xb = xin_ref[rows, :]
            h = _silu(_dot(xb, wg_ref[0])) * _dot(xb, wu_ref[0])
            out_ref[rows, :] += _dot((h * w[rows, :]).astype(BF16), wd_ref[0])

    tile(xb_ref, comb_ref, o_ref, sub)

    @pl.when(pl.program_id(0) == 0)
    def _():
        tile(xsb_ref, combs_ref, os_ref, xsb_ref.shape[0])


def _moe(xb, comb, xsb, combs, wg, wu, wd, tm):
    m, d = xb.shape
    ms = xsb.shape[0]
    n_exp, _, dexp = wg.shape
    return pl.pallas_call(
        functools.partial(_moe_kernel, sub=min(tm, MOE_SUB_ROWS)),
        grid=(m // tm, n_exp),
        in_specs=[
            pl.BlockSpec((tm, d), lambda i, e: (i, 0)),
            pl.BlockSpec((tm, n_exp), lambda i, e: (i, 0)),
            pl.BlockSpec((ms, d), lambda i, e: (0, 0)),
            pl.BlockSpec((ms, n_exp), lambda i, e: (0, 0)),
            pl.BlockSpec((1, d, dexp), lambda i, e: (e, 0, 0)),
            pl.BlockSpec((1, d, dexp), lambda i, e: (e, 0, 0)),
            pl.BlockSpec((1, dexp, d), lambda i, e: (e, 0, 0)),
        ],
        out_specs=[pl.BlockSpec((tm, d), lambda i, e: (i, 0)), pl.BlockSpec((ms, d), lambda i, e: (0, 0))],
        out_shape=[jax.ShapeDtypeStruct((m, d), F32), jax.ShapeDtypeStruct((ms, d), F32)],
        compiler_params=_cparams("arbitrary", "arbitrary"),
        name="moe",
    )(xb, comb, xsb, combs, wg, wu, wd)


def kernel(x_prompt, x_sample, state_conv, cache_k, cache_v, page_table, p_prompt, p_sample,
           conv_w_pw1, conv_b_pw1, conv_w_dw, conv_b_dw, conv_ln_g, conv_ln_b, conv_w_pw2,
           kv_w_k, kv_w_v, attn_w_q, attn_lambda, attn_sub_g, attn_w_o,
           ffn_w_gate, ffn_w_up, ffn_w_down, moe_w_router, moe_w_gate, moe_w_up, moe_w_down,
           post_ln_g, post_ln_b, ple_w_proj, ple_w_gate):
    bp, sp, d = x_prompt.shape
    bd, sd, _ = x_sample.shape
    depth = post_ln_g.shape[0]
    n_a = conv_w_pw1.shape[0]
    assert bp == 1 and sd == 1 and depth == 2 and n_a == 1
    n_heads = d // V_DIM
    n_maps = 2 * n_heads
    past = page_table.shape[1] * cache_k.shape[1]
    alpha = (2.0 * depth) ** 0.25
    lam_init = 0.8 - 0.6 * math.exp(-0.3 * 1)

    bf = lambda w: w.astype(BF16)
    row = lambda v: v.reshape(1, -1)

    xp = x_prompt.reshape(sp, d)
    xd = x_sample.reshape(bd, d)
    pp = p_prompt.reshape(depth, sp, -1)
    pdm = p_sample.reshape(depth, bd, -1)

    w1, w2 = bf(conv_w_pw1[0]), bf(conv_w_pw2[0])
    b1 = row(conv_b_pw1[0])
    conv_args = (conv_w_dw[0], row(conv_b_dw[0]), row(conv_ln_g[0]), row(conv_ln_b[0]), w2,
                 row(post_ln_g[0, 0]), row(post_ln_b[0, 0]), alpha)
    up = _glu(xp, w1, b1, 1024, 512)
    ud = _glu(xd, w1, b1, bd, 512)
    conv_prompt = up[sp - (CONV_W - 1):].reshape(1, 1, CONV_W - 1, d)
    hist = jnp.transpose(state_conv[0], (1, 0, 2))
    conv_sample = jnp.transpose(jnp.concatenate([hist[1:], ud[None]], axis=0), (1, 0, 2))[None]
    xp = _conv_prompt(up, xp, *conv_args, 256)
    xd = _conv_sample(hist, ud, xd, *conv_args)

    fw = (bf(ffn_w_gate[0]), bf(ffn_w_up[0]), bf(ffn_w_down[0]), row(post_ln_g[0, 1]), row(post_ln_b[0, 1]), alpha)
    xp, xd = _swiglu(xp, xd, *fw, 512, 512)

    pw = (bf(ple_w_gate[0]), bf(ple_w_proj[0]))
    xp = _ple(xp, pp[0], *pw, 1024, 512)
    xd = _ple(xd, pdm[0], *pw, bd, 512)

    wk, wv, wq = bf(kv_w_k), bf(kv_w_v), bf(attn_w_q[0])
    cos_p, sin_p = _rope_tables(jnp.arange(sp))
    cos_d, sin_d = _rope_tables(jnp.full((bd,), past))
    k_p, v_p, kb_p, qt_p, vt_p = _proj(xp, wk, wv, wq, cos_p, sin_p, 512, 512, True)
    k_d, v_d, qb_d = _proj(xd, wk, wv, wq, cos_d, sin_d, bd, 512, False)

    lam_p = attn_lambda[0]
    sub_g = row(attn_sub_g[0])
    ap = _attn_prompt(qt_p, kb_p, vt_p, lam_p, attn_sub_g[0].reshape(V_DIM, 1), lam_init, 512, 512)
    cache_kt = jnp.transpose(cache_k, (0, 2, 3, 1))
    qk3 = jnp.concatenate([qb_d.astype(F32).reshape(bd, n_maps, HEAD_DIM),
                           k_d.astype(BF16).astype(F32).reshape(bd, n_maps, HEAD_DIM)], axis=-1)
    ad = _attn_sample(page_table, qk3, v_d.reshape(bd, n_heads, V_DIM), lam_p, sub_g,
                      cache_kt, cache_v, lam_init, 8).reshape(bd, d)
    wo_args = (bf(attn_w_o[0]), row(post_ln_g[1, 0]), row(post_ln_b[1, 0]), moe_w_router[0], alpha)
    xp, xbp, combp = _wo(ap, xp, *wo_args, 256)
    xd, xbd, combd = _wo(ad, xd, *wo_args, bd)

    mw = (bf(moe_w_gate[0]), bf(moe_w_up[0]), bf(moe_w_down[0]))
    fp, fd = _moe(xbp, combp, xbd, combd, *mw, 512)

    pw = (bf(ple_w_gate[1]), bf(ple_w_proj[1]))
    post = (row(post_ln_g[1, 1]), row(post_ln_b[1, 1]), alpha)
    xp = _ple(xp, pp[1], *pw, 512, 512, post=(fp,) + post)
    xd = _ple(xd, pdm[1], *pw, bd, 512, post=(fd,) + post)

    return (xp.reshape(bp, sp, d), xd.reshape(bd, sd, d), conv_prompt, conv_sample,
            k_p.reshape(bp, sp, n_maps, HEAD_DIM), v_p.reshape(bp, sp, n_heads, V_DIM),
            k_d.reshape(bd, sd, n_maps, HEAD_DIM), v_d.reshape(bd, sd, n_heads, V_DIM))
```

```python
import functools
import math

import jax
import jax.numpy as jnp
from jax import lax
from jax.experimental import pallas as pl
from jax.experimental.pallas import tpu as pltpu

HEAD_DIM = 64
V_DIM = 2 * HEAD_DIM
CONV_W = 31
TOP_K = 2
ROPE_THETA = 10000.0
LN_EPS = 1e-5
ATTN_SCALE = HEAD_DIM ** -0.5
LOG2E = math.log2(math.e)

LANES = 128
SUBLANES = 8
MOE_SUB_ROWS = 256
HALO = 32
VMEM_LIMIT = 56 * 1024 * 1024

F32 = jnp.float32
BF16 = jnp.bfloat16


def _cparams(*sem):
    return pltpu.CompilerParams(dimension_semantics=sem, vmem_limit_bytes=VMEM_LIMIT)


def _dot(a, b):
    return jnp.dot(a, b, preferred_element_type=F32)


def _ln(x, g, b):
    mu = jnp.mean(x, axis=-1, keepdims=True)
    xc = x - mu
    var = jnp.mean(xc * xc, axis=-1, keepdims=True)
    return xc * lax.rsqrt(var + LN_EPS) * g + b


def _sigmoid(x):
    return 1.0 / (1.0 + jnp.exp(-x))


def _silu(x):
    return x * _sigmoid(x)


def _glu_kernel(x_ref, xs_ref, wa_ref, wg_ref, ba_ref, bg_ref, u_ref, us_ref, xb_ref, xsb_ref):
    j = pl.program_id(1)
    tn = u_ref.shape[1]

    def tile(xin_ref, xbf_ref):
        @pl.when(j == 0)
        def _():
            xbf_ref[...] = xin_ref[...].astype(BF16)

        xb = xbf_ref[...]
        a = _dot(xb, wa_ref[...]) + ba_ref[...]
        g = _dot(xb, wg_ref[...]) + bg_ref[...]
        return a * _sigmoid(g)

    u_ref[...] = tile(x_ref, xb_ref)

    @pl.when(pl.program_id(0) == 0)
    def _():
        us_ref[:, pl.ds(pl.multiple_of(j * tn, tn), tn)] = tile(xs_ref, xsb_ref)


def _glu(x, xs, w1, b1, tm, tn):
    m, d = x.shape
    ms = xs.shape[0]
    nj = d // tn
    small = pl.BlockSpec((ms, d), lambda i, j: (0, 0))
    return pl.pallas_call(
        _glu_kernel,
        grid=(m // tm, nj),
        in_specs=[
            pl.BlockSpec((tm, d), lambda i, j: (i, 0)),
            small,
            pl.BlockSpec((d, tn), lambda i, j: (0, j)),
            pl.BlockSpec((d, tn), lambda i, j: (0, j + nj)),
            pl.BlockSpec((1, tn), lambda i, j: (0, j)),
            pl.BlockSpec((1, tn), lambda i, j: (0, j + nj)),
        ],
        out_specs=[pl.BlockSpec((tm, tn), lambda i, j: (i, j)), small],
        out_shape=[jax.ShapeDtypeStruct((m, d), F32), jax.ShapeDtypeStruct((ms, d), F32)],
        scratch_shapes=[pltpu.VMEM((tm, d), BF16), pltpu.VMEM((ms, d), BF16)],
        compiler_params=_cparams("arbitrary", "arbitrary"),
        name="glu",
    )(x, xs, w1, w1, b1, b1)


def _mix_tail(y, x, g_ref, b_ref, w2_ref, pg_ref, pb_ref, alpha):
    y = _silu(_ln(y, g_ref[...], b_ref[...]))
    m = _dot(y.astype(BF16), w2_ref[...])
    return _ln(alpha * x + m, pg_ref[...], pb_ref[...])


def _conv_prompt_kernel(uprev_ref, u_ref, x_ref, wdw_ref, bdw_ref, g_ref, b_ref, w2_ref,
                        pg_ref, pb_ref, o_ref, full_ref, y_ref, *, alpha, rows):
    tm, d = u_ref.shape
    first = pl.program_id(0) == 0
    full_ref[0:HALO, :] = jnp.where(first, 0.0, uprev_ref[...])
    full_ref[HALO:, :] = u_ref[...]
    off = HALO - (CONV_W - 1)

    def col_body(c, carry):
        cs = pl.ds(pl.multiple_of(c * LANES, LANES), LANES)
        w = wdw_ref[:, cs]
        bias = bdw_ref[:, cs]

        def row_body(r, carry2):
            base = pl.multiple_of(r * rows, rows)
            win = full_ref[pl.ds(base, rows + HALO), cs]
            acc = jnp.zeros((rows, LANES), F32)
            for res in range(SUBLANES):
                sh = win if res == 0 else pltpu.roll(win, rows + HALO - res, 0)
                for t in range(CONV_W):
                    if (off + t) % SUBLANES == res:
                        a8 = off + t - res
                        acc = acc + sh[a8:a8 + rows, :] * w[t:t + 1, :]
            y_ref[pl.ds(base, rows), cs] = acc + bias
            return carry2

        return lax.fori_loop(0, tm // rows, row_body, carry)

    lax.fori_loop(0, d // LANES, col_body, 0)
    o_ref[...] = _mix_tail(y_ref[...], x_ref[...], g_ref, b_ref, w2_ref, pg_ref, pb_ref, alpha)


def _conv_prompt(u, x, wdw, bdw, g, b, w2, pg, pb, alpha, tm):
    m, d = x.shape
    nh = tm // HALO
    vec = pl.BlockSpec((1, d), lambda i: (0, 0))
    return pl.pallas_call(
        functools.partial(_conv_prompt_kernel, alpha=alpha, rows=64),
        grid=(m // tm,),
        in_specs=[
            pl.BlockSpec((HALO, d), lambda i: (jnp.maximum(i * nh - 1, 0), 0)),
            pl.BlockSpec((tm, d), lambda i: (i, 0)),
            pl.BlockSpec((tm, d), lambda i: (i, 0)),
            pl.BlockSpec((CONV_W, d), lambda i: (0, 0)),
            vec, vec, vec,
            pl.BlockSpec((d, d), lambda i: (0, 0)),
            vec, vec,
        ],
        out_specs=pl.BlockSpec((tm, d), lambda i: (i, 0)),
        out_shape=jax.ShapeDtypeStruct((m, d), F32),
        scratch_shapes=[pltpu.VMEM((tm + HALO, d), F32), pltpu.VMEM((tm, d), F32)],
        compiler_params=_cparams("parallel"),
        name="conv_prompt",
    )(u, u, x, wdw, bdw, g, b, w2, pg, pb)


def _conv_sample_kernel(hist_ref, u_ref, x_ref, wdw_ref, bdw_ref, g_ref, b_ref, w2_ref,
                        pg_ref, pb_ref, o_ref, *, alpha):
    y = u_ref[...] * wdw_ref[CONV_W - 1:CONV_W, :]
    for t in range(CONV_W - 1):
        y = y + hist_ref[t] * wdw_ref[t:t + 1, :]
    y = y + bdw_ref[...]
    o_ref[...] = _mix_tail(y, x_ref[...], g_ref, b_ref, w2_ref, pg_ref, pb_ref, alpha)


def _conv_sample(hist, u, x, wdw, bdw, g, b, w2, pg, pb, alpha):
    m, d = x.shape
    return pl.pallas_call(
        functools.partial(_conv_sample_kernel, alpha=alpha),
        out_shape=jax.ShapeDtypeStruct((m, d), F32),
        compiler_params=pltpu.CompilerParams(vmem_limit_bytes=VMEM_LIMIT),
        name="conv_sample",
    )(hist, u, x, wdw, bdw, g, b, w2, pg, pb)


def _swiglu_kernel(x_ref, xs_ref, wg_ref, wu_ref, wd_ref, pg_ref, pb_ref, o_ref, os_ref, xb_ref, xsb_ref,
                   *, alpha):
    f = pl.program_id(1)

    def tile(xin_ref, out_ref, xbf_ref):
        @pl.when(f == 0)
        def _():
            xbf_ref[...] = xin_ref[...].astype(BF16)

        xb = xbf_ref[...]
        h = _silu(_dot(xb, wg_ref[...])) * _dot(xb, wu_ref[...])
        part = _dot(h.astype(BF16), wd_ref[...])

        @pl.when(f == 0)
        def _():
            out_ref[...] = part

        @pl.when(f > 0)
        def _():
            out_ref[...] += part

        @pl.when(f == pl.num_programs(1) - 1)
        def _():
            out_ref[...] = _ln(alpha * xin_ref[...] + out_ref[...], pg_ref[...], pb_ref[...])

    tile(x_ref, o_ref, xb_ref)

    @pl.when(pl.program_id(0) == 0)
    def _():
        tile(xs_ref, os_ref, xsb_ref)


def _swiglu(x, xs, wg, wu, wd, pg, pb, alpha, tm, tf):
    m, d = x.shape
    ms = xs.shape[0]
    dff = wg.shape[1]
    vec = pl.BlockSpec((1, d), lambda i, f: (0, 0))
    small = pl.BlockSpec((ms, d), lambda i, f: (0, 0))
    return pl.pallas_call(
        functools.partial(_swiglu_kernel, alpha=alpha),
        grid=(m // tm, dff // tf),
        in_specs=[
            pl.BlockSpec((tm, d), lambda i, f: (i, 0)),
            small,
            pl.BlockSpec((d, tf), lambda i, f: (0, f)),
            pl.BlockSpec((d, tf), lambda i, f: (0, f)),
            pl.BlockSpec((tf, d), lambda i, f: (f, 0)),
            vec, vec,
        ],
        out_specs=[pl.BlockSpec((tm, d), lambda i, f: (i, 0)), small],
        out_shape=[jax.ShapeDtypeStruct((m, d), F32), jax.ShapeDtypeStruct((ms, d), F32)],
        scratch_shapes=[pltpu.VMEM((tm, d), BF16), pltpu.VMEM((ms, d), BF16)],
        compiler_params=_cparams("arbitrary", "arbitrary"),
        name="swiglu",
    )(x, xs, wg, wu, wd, pg, pb)


def _ple_kernel(*refs, alpha, post_ln):
    if post_ln:
        x_ref, f_ref, pg_ref, pb_ref, p_ref, wg_ref, wp_ref, o_ref, xb_ref, xf_ref = refs
    else:
        x_ref, p_ref, wg_ref, wp_ref, o_ref, xb_ref = refs
        xf_ref = x_ref
    j = pl.program_id(1)
    tn = o_ref.shape[1]

    @pl.when(j == 0)
    def _():
        if post_ln:
            xf_ref[...] = _ln(alpha * x_ref[...] + f_ref[...], pg_ref[...], pb_ref[...])
        xb_ref[...] = xf_ref[...].astype(BF16)

    gate = _sigmoid(_dot(xb_ref[...], wg_ref[...]))
    xc = xf_ref[:, pl.ds(pl.multiple_of(j * tn, tn), tn)]
    o_ref[...] = xc + gate * _dot(p_ref[...].astype(BF16), wp_ref[...])


def _ple(x, p, wg, wp, tm, tn, post=None):
    m, d = x.shape
    pd = p.shape[1]
    row = pl.BlockSpec((tm, d), lambda i, j: (i, 0))
    vec = pl.BlockSpec((1, d), lambda i, j: (0, 0))
    tail_specs = [pl.BlockSpec((tm, pd), lambda i, j: (i, 0)),
                  pl.BlockSpec((d, tn), lambda i, j: (0, j)),
                  pl.BlockSpec((pd, tn), lambda i, j: (0, j))]
    scratch = [pltpu.VMEM((tm, d), BF16)]
    if post is None:
        args, in_specs, alpha = (x, p, wg, wp), [row] + tail_specs, 1.0
    else:
        f, pg, pb, alpha = post
        args, in_specs = (x, f, pg, pb, p, wg, wp), [row, row, vec, vec] + tail_specs
        scratch.append(pltpu.VMEM((tm, d), F32))
    return pl.pallas_call(
        functools.partial(_ple_kernel, alpha=alpha, post_ln=post is not None),
        grid=(m // tm, d // tn),
        in_specs=in_specs,
        out_specs=pl.BlockSpec((tm, tn), lambda i, j: (i, j)),
        out_shape=jax.ShapeDtypeStruct((m, d), F32),
        scratch_shapes=scratch,
        compiler_params=_cparams("parallel", "arbitrary"),
        name="ple",
    )(*args)


def _rope_tables(pos):
    half = HEAD_DIM // 2
    inv = ROPE_THETA ** (-jnp.arange(half, dtype=F32) / half)
    ang = pos.astype(F32)[:, None] * inv[None, :]
    cos, sin = jnp.cos(ang), jnp.sin(ang)
    cos_t = jnp.concatenate([cos, cos, cos, cos], axis=-1)
    sin_t = jnp.concatenate([-sin, sin, -sin, sin], axis=-1)
    return cos_t, sin_t


def _rope_tile(h, cos, sin, first_half):
    partner = jnp.where(first_half, pltpu.roll(h, LANES - HEAD_DIM // 2, 1),
                        pltpu.roll(h, HEAD_DIM // 2, 1))
    return h * cos + partner * sin


def _proj_kernel(x_ref, wk_ref, wv_ref, wq_ref, cos_ref, sin_ref, *refs, transposed):
    if transposed:
        k_ref, v_ref, kb_ref, qt_ref, vt_ref, xb_ref = refs
    else:
        k_ref, v_ref, qb_ref, xb_ref = refs

    @pl.when(pl.program_id(1) == 0)
    def _():
        xb_ref[...] = x_ref[...].astype(BF16)

    xb = xb_ref[...]
    tm, tn = k_ref.shape
    cos, sin = cos_ref[...], sin_ref[...]
    lane = lax.broadcasted_iota(jnp.int32, (tm, LANES), 1)
    first_half = (lane % HEAD_DIM) < (HEAD_DIM // 2)
    hk = _dot(xb, wk_ref[...])
    hq = _dot(xb, wq_ref[...])
    for c in range(tn // LANES):
        cs = slice(c * LANES, (c + 1) * LANES)
        kr = _rope_tile(hk[:, cs], cos, sin, first_half)
        k_ref[:, cs] = kr
        qr = _rope_tile(hq[:, cs], cos, sin, first_half)
        if transposed:
            kb_ref[:, cs] = kr.astype(BF16)
            qt_ref[cs, :] = (qr * (ATTN_SCALE * LOG2E)).T.astype(BF16)
        else:
            qb_ref[:, cs] = (qr * ATTN_SCALE).astype(BF16)
    hv = _dot(xb, wv_ref[...])
    v_ref[...] = hv
    if transposed:
        vt_ref[...] = hv.T.astype(BF16)


def _proj(x, wk, wv, wq, cos_t, sin_t, tm, tn, transposed):
    m, d = x.shape
    wspec = pl.BlockSpec((d, tn), lambda i, j: (0, j))
    tspec = pl.BlockSpec((tm, LANES), lambda i, j: (i, 0))
    ospec = pl.BlockSpec((tm, tn), lambda i, j: (i, j))
    tr_spec = pl.BlockSpec((tn, tm), lambda i, j: (j, i))
    f32_out = jax.ShapeDtypeStruct((m, d), F32)
    if transposed:
        out_specs = [ospec, ospec, ospec, tr_spec, tr_spec]
        out_shape = [f32_out, f32_out, jax.ShapeDtypeStruct((m, d), BF16),
                     jax.ShapeDtypeStruct((d, m), BF16), jax.ShapeDtypeStruct((d, m), BF16)]
    else:
        out_specs = [ospec, ospec, ospec]
        out_shape = [f32_out, f32_out, jax.ShapeDtypeStruct((m, d), BF16)]
    return pl.pallas_call(
        functools.partial(_proj_kernel, transposed=transposed),
        grid=(m // tm, d // tn),
        in_specs=[pl.BlockSpec((tm, d), lambda i, j: (i, 0)), wspec, wspec, wspec, tspec, tspec],
        out_specs=out_specs,
        out_shape=out_shape,
        scratch_shapes=[pltpu.VMEM((tm, d), BF16)],
        compiler_params=_cparams("parallel", "arbitrary"),
        name="proj",
    )(x, wk, wv, wq, cos_t, sin_t)


def _diff_lambda(lam_ref, lam_init):
    lp = lam_ref[...]
    s1 = jnp.sum(lp[0:1, :] * lp[1:2, :], axis=-1, keepdims=True)
    s2 = jnp.sum(lp[2:3, :] * lp[3:4, :], axis=-1, keepdims=True)
    return jnp.exp(s1) - jnp.exp(s2) + lam_init


def _diff_finalize(o1, o2, lam, g, lam_init):
    o = o1 - lam * o2
    o = o * lax.rsqrt(jnp.mean(o * o, axis=-1, keepdims=True) + LN_EPS)
    return o * g * (1.0 - lam_init)


def _attn_prompt_kernel(qt_ref, k_ref, vt_ref, lam_ref, g_ref, o_ref,
                        sa_ref, sb_ref, m_ref, l_ref, acc_ref, *, lam_init, tk):
    tq = qt_ref.shape[1]
    i = pl.program_id(1)
    q0 = i * tq
    n_full = q0 // tk
    qt = qt_ref[...]
    feat = lax.broadcasted_iota(jnp.int32, qt.shape, 0)
    qz = jnp.concatenate([jnp.where(feat < HEAD_DIM, qt, 0), jnp.where(feat >= HEAD_DIM, qt, 0)], axis=1)
    m_ref[...] = jnp.full(m_ref.shape, -jnp.inf, F32)
    l_ref[...] = jnp.zeros(l_ref.shape, F32)
    acc_ref[...] = jnp.zeros(acc_ref.shape, F32)

    def scores(b, dst_ref):
        base = pl.multiple_of(b * tk, tk)
        dst_ref[...] = _dot(k_ref[pl.ds(base, tk), :], qz)

    def softmax_pv(b, src_ref, masked):
        base = pl.multiple_of(b * tk, tk)
        s = src_ref[...]
        if masked:
            key = base + lax.broadcasted_iota(jnp.int32, s.shape, 0)
            qry = q0 + lax.broadcasted_iota(jnp.int32, s.shape, 1) % tq
            s = jnp.where(key <= qry, s, -jnp.inf)
        m_old = m_ref[...]
        m_new = jnp.maximum(m_old, jnp.max(s, axis=0, keepdims=True))
        a = jnp.exp2(m_old - m_new)
        p = jnp.exp2(s - m_new)
        l_ref[...] = a * l_ref[...] + jnp.sum(p, axis=0, keepdims=True)
        m_ref[...] = m_new
        acc_ref[...] = a * acc_ref[...] + _dot(vt_ref[:, pl.ds(base, tk)], p.astype(BF16))

    scores(0, sa_ref)

    def pair(t, carry):
        b = 2 * t
        scores(b + 1, sb_ref)
        softmax_pv(b, sa_ref, False)
        scores(b + 2, sa_ref)
        softmax_pv(b + 1, sb_ref, False)
        return carry

    lax.fori_loop(0, n_full // 2, pair, 0)

    @pl.when(n_full % 2 == 1)
    def _():
        scores(n_full, sb_ref)
        softmax_pv(n_full - 1, sa_ref, False)
        softmax_pv(n_full, sb_ref, True)

    @pl.when(n_full % 2 == 0)
    def _():
        softmax_pv(n_full, sa_ref, True)

    lam = _diff_lambda(lam_ref, lam_init)
    o = acc_ref[...] * (1.0 / l_ref[...])
    ot = o[:, :tq] - lam * o[:, tq:]
    ot = ot * lax.rsqrt(jnp.mean(ot * ot, axis=0, keepdims=True) + LN_EPS)
    ot = ot * g_ref[...] * (1.0 - lam_init)
    o_ref[...] = ot.T.astype(o_ref.dtype)


def _attn_prompt(qt, kb, vt, lam_p, sub_g_col, lam_init, tq, tk):
    d, s = qt.shape
    nh = d // V_DIM
    return pl.pallas_call(
        functools.partial(_attn_prompt_kernel, lam_init=lam_init, tk=tk),
        grid=(nh, s // tq),
        in_specs=[
            pl.BlockSpec((V_DIM, tq), lambda h, i: (h, i)),
            pl.BlockSpec((s, V_DIM), lambda h, i: (0, h)),
            pl.BlockSpec((V_DIM, s), lambda h, i: (h, 0)),
            pl.BlockSpec(lam_p.shape, lambda h, i: (0, 0)),
            pl.BlockSpec((V_DIM, 1), lambda h, i: (0, 0)),
        ],
        out_specs=pl.BlockSpec((tq, V_DIM), lambda h, i: (i, h)),
        out_shape=jax.ShapeDtypeStruct((s, d), BF16),
        scratch_shapes=[pltpu.VMEM((tk, 2 * tq), F32), pltpu.VMEM((tk, 2 * tq), F32),
                        pltpu.VMEM((1, 2 * tq), F32), pltpu.VMEM((1, 2 * tq), F32),
                        pltpu.VMEM((V_DIM, 2 * tq), F32)],
        compiler_params=_cparams("parallel", "arbitrary"),
        name="attn_prompt",
    )(qt, kb, vt, lam_p, sub_g_col)


def _attn_sample_kernel(pt_ref, qk_ref, vn_ref, lam_ref, g_ref, e_ref, *refs,
                        lam_init, pages_per_step):
    k_refs = refs[:pages_per_step]
    v_refs = refs[pages_per_step:2 * pages_per_step]
    o_ref = refs[2 * pages_per_step]
    qb_ref, m_ref, l_ref, acc_ref = refs[2 * pages_per_step + 1:]
    step_idx = pl.program_id(1)
    n_maps, _, page = qb_ref.shape
    n_heads = n_maps // 2

    @pl.when(step_idx == 0)
    def _():
        qk = qk_ref[0]
        qt = qk.T
        for mp in range(n_maps):
            qb_ref[mp] = jnp.broadcast_to(qt[:HEAD_DIM, mp:mp + 1], (HEAD_DIM, page))
        m_ref[...] = jnp.sum(qk[:, :HEAD_DIM] * qk[:, HEAD_DIM:], axis=-1, keepdims=True)
        l_ref[...] = jnp.ones_like(l_ref)
        vn = vn_ref[0]
        for h in range(n_heads):
            acc_ref[2 * h:2 * h + 2, :] = jnp.broadcast_to(vn[h:h + 1, :], (2, V_DIM))

    slot = lax.broadcasted_iota(jnp.int32, (n_maps, page * n_heads), 1)
    mp_row = lax.broadcasted_iota(jnp.int32, (n_maps, page * n_heads), 0)
    own_head = (slot % n_heads) == (mp_row >> 1)
    for kp_ref, vp_ref in zip(k_refs, v_refs):
        s = jnp.sum(kp_ref[0] * qb_ref[...], axis=1)
        m_old = m_ref[...]
        m_new = jnp.maximum(m_old, jnp.max(s, axis=-1, keepdims=True))
        a = jnp.exp(m_old - m_new)
        p = jnp.exp(s - m_new)
        l_ref[...] = a * l_ref[...] + jnp.sum(p, axis=-1, keepdims=True)
        m_ref[...] = m_new
        pe = _dot(p.astype(BF16), e_ref[...])
        pm = jnp.where(own_head, pe, 0.0).astype(BF16)
        vb = vp_ref[0].reshape(page * n_heads, V_DIM).astype(BF16)
        acc_ref[...] = a * acc_ref[...] + _dot(pm, vb)

    @pl.when(step_idx == pl.num_programs(1) - 1)
    def _():
        acc_ref[...] = acc_ref[...] / l_ref[...]
        o1 = acc_ref[pl.ds(0, n_heads, stride=2), :]
        o2 = acc_ref[pl.ds(1, n_heads, stride=2), :]
        lam = _diff_lambda(lam_ref, lam_init)
        o_ref[0] = _diff_finalize(o1, o2, lam, g_ref[...], lam_init).astype(o_ref.dtype)


def _attn_sample(page_table, qk3, vn3, lam_p, sub_g, cache_kt, cache_v, lam_init, pages_per_step):
    nb, n_pages = page_table.shape
    _, n_maps, hd, page = cache_kt.shape
    n_heads = cache_v.shape[2]
    pps = pages_per_step

    def kspec(j):
        return pl.BlockSpec((1, n_maps, hd, page), lambda b, s, pt: (pt[b, s * pps + j], 0, 0, 0))

    def vspec(j):
        return pl.BlockSpec((1, page, n_heads, V_DIM), lambda b, s, pt: (pt[b, s * pps + j], 0, 0, 0))

    expand = (jnp.arange(page * n_heads)[None, :] // n_heads == jnp.arange(page)[:, None]).astype(BF16)
    grid_spec = pltpu.PrefetchScalarGridSpec(
        num_scalar_prefetch=1,
        grid=(nb, n_pages // pps),
        in_specs=[
            pl.BlockSpec((1, n_maps, 2 * hd), lambda b, s, pt: (b, 0, 0)),
            pl.BlockSpec((1, n_heads, V_DIM), lambda b, s, pt: (b, 0, 0)),
            pl.BlockSpec(lam_p.shape, lambda b, s, pt: (0, 0)),
            pl.BlockSpec((1, V_DIM), lambda b, s, pt: (0, 0)),
            pl.BlockSpec(expand.shape, lambda b, s, pt: (0, 0)),
        ] + [kspec(j) for j in range(pps)] + [vspec(j) for j in range(pps)],
        out_specs=pl.BlockSpec((1, n_heads, V_DIM), lambda b, s, pt: (b, 0, 0)),
        scratch_shapes=[
            pltpu.VMEM((n_maps, hd, page), F32),
            pltpu.VMEM((n_maps, 1), F32),
            pltpu.VMEM((n_maps, 1), F32),
            pltpu.VMEM((n_maps, V_DIM), F32),
        ],
    )
    return pl.pallas_call(
        functools.partial(_attn_sample_kernel, lam_init=lam_init, pages_per_step=pps),
        grid_spec=grid_spec,
        out_shape=jax.ShapeDtypeStruct((nb, n_heads, V_DIM), BF16),
        compiler_params=_cparams("parallel", "arbitrary"),
        name="attn_sample",
    )(page_table, qk3, vn3, lam_p, sub_g, expand, *([cache_kt] * pps), *([cache_v] * pps))


def _route(xb, wr):
    n_exp = wr.shape[1]
    logits = _dot(xb, wr.astype(BF16))
    idx = lax.broadcasted_iota(jnp.int32, logits.shape, 1)
    v1 = jnp.max(logits, axis=-1, keepdims=True)
    i1 = jnp.min(jnp.where(logits == v1, idx, n_exp), axis=-1, keepdims=True)
    rest = jnp.where(idx == i1, -jnp.inf, logits)
    v2 = jnp.max(rest, axis=-1, keepdims=True)
    i2 = jnp.min(jnp.where(rest == v2, idx, n_exp), axis=-1, keepdims=True)
    t = jnp.exp(v2 - v1)
    g1 = 1.0 / (1.0 + t)
    g2 = t / (1.0 + t)
    return jnp.where(idx == i1, g1, 0.0) + jnp.where(idx == i2, g2, 0.0)


def _wo_kernel(a_ref, x_ref, as_ref, xs_ref, wo_ref, pg_ref, pb_ref, wr_ref,
               o_ref, xb_ref, comb_ref, os_ref, xsb_ref, combs_ref, *, alpha):
    def tile(ain_ref, xin_ref, out_ref, xbf_ref, cout_ref):
        m = _dot(ain_ref[...], wo_ref[...])
        x1 = _ln(alpha * xin_ref[...] + m, pg_ref[...], pb_ref[...])
        out_ref[...] = x1
        xb = x1.astype(BF16)
        xbf_ref[...] = xb
        cout_ref[...] = _route(xb, wr_ref[...])

    tile(a_ref, x_ref, o_ref, xb_ref, comb_ref)

    @pl.when(pl.program_id(0) == 0)
    def _():
        tile(as_ref, xs_ref, os_ref, xsb_ref, combs_ref)


def _wo(a, x, a_s, xs, wo, pg, pb, wr, alpha, tm):
    m, d = x.shape
    ms = xs.shape[0]
    n_exp = wr.shape[1]
    vec = pl.BlockSpec((1, d), lambda i: (0, 0))
    row = pl.BlockSpec((tm, d), lambda i: (i, 0))
    small = pl.BlockSpec((ms, d), lambda i: (0, 0))
    return pl.pallas_call(
        functools.partial(_wo_kernel, alpha=alpha),
        grid=(m // tm,),
        in_specs=[row, row, small, small, pl.BlockSpec((d, d), lambda i: (0, 0)), vec, vec,
                  pl.BlockSpec((d, n_exp), lambda i: (0, 0))],
        out_specs=[row, row, pl.BlockSpec((tm, n_exp), lambda i: (i, 0)),
                   small, small, pl.BlockSpec((ms, n_exp), lambda i: (0, 0))],
        out_shape=[jax.ShapeDtypeStruct((m, d), F32), jax.ShapeDtypeStruct((m, d), BF16),
                   jax.ShapeDtypeStruct((m, n_exp), F32),
                   jax.ShapeDtypeStruct((ms, d), F32), jax.ShapeDtypeStruct((ms, d), BF16),
                   jax.ShapeDtypeStruct((ms, n_exp), F32)],
        compiler_params=_cparams("arbitrary"),
        name="wo",
    )(a, x, a_s, xs, wo, pg, pb, wr)


def _moe_kernel(xb_ref, comb_ref, xsb_ref, combs_ref, wg_ref, wu_ref, wd_ref, o_ref, os_ref, *, sub):
    e = pl.program_id(1)

    def tile(xin_ref, cin_ref, out_ref, step):
        comb = cin_ref[...]
        idx = lax.broadcasted_iota(jnp.int32, comb.shape, 1)
        w = jnp.sum(jnp.where(idx == e, comb, 0.0), axis=-1, keepdims=True)

        @pl.when(e == 0)
        def _():
            out_ref[...] = jnp.zeros(out_ref.shape, F32)

        for r0 in range(0, xin_ref.shape[0], step):
            rows = slice(r0, r0 + step)
            xb = xin_ref[rows, :]
            h = _silu(_dot(xb, wg_ref[0])) * _dot(xb, wu_ref[0])
            out_ref[rows, :] += _dot((h * w[rows, :]).astype(BF16), wd_ref[0])

    tile(xb_ref, comb_ref, o_ref, sub)

    @pl.when(pl.program_id(0) == 0)
    def _():
        tile(xsb_ref, combs_ref, os_ref, xsb_ref.shape[0])


def _moe(xb, comb, xsb, combs, wg, wu, wd, tm):
    m, d = xb.shape
    ms = xsb.shape[0]
    n_exp, _, dexp = wg.shape
    return pl.pallas_call(
        functools.partial(_moe_kernel, sub=min(tm, MOE_SUB_ROWS)),
        grid=(m // tm, n_exp),
        in_specs=[
            pl.BlockSpec((tm, d), lambda i, e: (i, 0)),
            pl.BlockSpec((tm, n_exp), lambda i, e: (i, 0)),
            pl.BlockSpec((ms, d), lambda i, e: (0, 0)),
            pl.BlockSpec((ms, n_exp), lambda i, e: (0, 0)),
            pl.BlockSpec((1, d, dexp), lambda i, e: (e, 0, 0)),
            pl.BlockSpec((1, d, dexp), lambda i, e: (e, 0, 0)),
            pl.BlockSpec((1, dexp, d), lambda i, e: (e, 0, 0)),
        ],
        out_specs=[pl.BlockSpec((tm, d), lambda i, e: (i, 0)), pl.BlockSpec((ms, d), lambda i, e: (0, 0))],
        out_shape=[jax.ShapeDtypeStruct((m, d), F32), jax.ShapeDtypeStruct((ms, d), F32)],
        compiler_params=_cparams("arbitrary", "arbitrary"),
        name="moe",
    )(xb, comb, xsb, combs, wg, wu, wd)


def kernel(x_prompt, x_sample, state_conv, cache_k, cache_v, page_table, p_prompt, p_sample,
           conv_w_pw1, conv_b_pw1, conv_w_dw, conv_b_dw, conv_ln_g, conv_ln_b, conv_w_pw2,
           kv_w_k, kv_w_v, attn_w_q, attn_lambda, attn_sub_g, attn_w_o,
           ffn_w_gate, ffn_w_up, ffn_w_down, moe_w_router, moe_w_gate, moe_w_up, moe_w_down,
           post_ln_g, post_ln_b, ple_w_proj, ple_w_gate):
    bp, sp, d = x_prompt.shape
    bd, sd, _ = x_sample.shape
    depth = post_ln_g.shape[0]
    n_a = conv_w_pw1.shape[0]
    assert bp == 1 and sd == 1 and depth == 2 and n_a == 1
    n_heads = d // V_DIM
    n_maps = 2 * n_heads
    past = page_table.shape[1] * cache_k.shape[1]
    alpha = (2.0 * depth) ** 0.25
    lam_init = 0.8 - 0.6 * math.exp(-0.3 * 1)

    bf = lambda w: w.astype(BF16)
    row = lambda v: v.reshape(1, -1)

    xp = x_prompt.reshape(sp, d)
    xd = x_sample.reshape(bd, d)
    pp = p_prompt.reshape(depth, sp, -1)
    pdm = p_sample.reshape(depth, bd, -1)

    w1, w2 = bf(conv_w_pw1[0]), bf(conv_w_pw2[0])
    b1 = row(conv_b_pw1[0])
    conv_args = (conv_w_dw[0], row(conv_b_dw[0]), row(conv_ln_g[0]), row(conv_ln_b[0]), w2,
                 row(post_ln_g[0, 0]), row(post_ln_b[0, 0]), alpha)
    up, ud = _glu(xp, xd, w1, b1, 1024, 512)
    conv_prompt = up[sp - (CONV_W - 1):].reshape(1, 1, CONV_W - 1, d)
    hist = jnp.transpose(state_conv[0], (1, 0, 2))
    conv_sample = jnp.transpose(jnp.concatenate([hist[1:], ud[None]], axis=0), (1, 0, 2))[None]
    xp = _conv_prompt(up, xp, *conv_args, 256)
    xd = _conv_sample(hist, ud, xd, *conv_args)

    fw = (bf(ffn_w_gate[0]), bf(ffn_w_up[0]), bf(ffn_w_down[0]), row(post_ln_g[0, 1]), row(post_ln_b[0, 1]), alpha)
    xp, xd = _swiglu(xp, xd, *fw, 512, 512)

    pw = (bf(ple_w_gate[0]), bf(ple_w_proj[0]))
    xp = _ple(xp, pp[0], *pw, 1024, 512)
    xd = _ple(xd, pdm[0], *pw, bd, 512)

    wk, wv, wq = bf(kv_w_k), bf(kv_w_v), bf(attn_w_q[0])
    cos_p, sin_p = _rope_tables(jnp.arange(sp))
    cos_d, sin_d = _rope_tables(jnp.full((bd,), past))
    k_p, v_p, kb_p, qt_p, vt_p = _proj(xp, wk, wv, wq, cos_p, sin_p, 1024, 512, True)
    k_d, v_d, qb_d = _proj(xd, wk, wv, wq, cos_d, sin_d, bd, 512, False)

    lam_p = attn_lambda[0]
    sub_g = row(attn_sub_g[0])
    ap = _attn_prompt(qt_p, kb_p, vt_p, lam_p, attn_sub_g[0].reshape(V_DIM, 1), lam_init, 512, 512)
    cache_kt = jnp.transpose(cache_k, (0, 2, 3, 1))
    qk3 = jnp.concatenate([qb_d.astype(F32).reshape(bd, n_maps, HEAD_DIM),
                           k_d.astype(BF16).astype(F32).reshape(bd, n_maps, HEAD_DIM)], axis=-1)
    ad = _attn_sample(page_table, qk3, v_d.reshape(bd, n_heads, V_DIM), lam_p, sub_g,
                      cache_kt, cache_v, lam_init, 8).reshape(bd, d)
    wo_args = (bf(attn_w_o[0]), row(post_ln_g[1, 0]), row(post_ln_b[1, 0]), moe_w_router[0], alpha)
    xp, xbp, combp, xd, xbd, combd = _wo(ap, xp, ad, xd, *wo_args, 256)

    mw = (bf(moe_w_gate[0]), bf(moe_w_up[0]), bf(moe_w_down[0]))
    fp, fd = _moe(xbp, combp, xbd, combd, *mw, 512)

    pw = (bf(ple_w_gate[1]), bf(ple_w_proj[1]))
    post = (row(post_ln_g[1, 1]), row(post_ln_b[1, 1]), alpha)
    xp = _ple(xp, pp[1], *pw, 512, 512, post=(fp,) + post)
    xd = _ple(xd, pdm[1], *pw, bd, 512, post=(fd,) + post)

    return (xp.reshape(bp, sp, d), xd.reshape(bd, sd, d), conv_prompt, conv_sample,
            k_p.reshape(bp, sp, n_maps, HEAD_DIM), v_p.reshape(bp, sp, n_heads, V_DIM),
            k_d.reshape(bd, sd, n_maps, HEAD_DIM), v_d.reshape(bd, sd, n_heads, V_DIM))
```

```python
import functools
import math

import jax
import jax.numpy as jnp
from jax import lax
from jax.experimental import pallas as pl
from jax.experimental.pallas import tpu as pltpu

HEAD_DIM = 64
V_DIM = 2 * HEAD_DIM
CONV_W = 31
TOP_K = 2
ROPE_THETA = 10000.0
LN_EPS = 1e-5
ATTN_SCALE = HEAD_DIM ** -0.5
LOG2E = math.log2(math.e)

LANES = 128
SUBLANES = 8
MOE_SUB_ROWS = 256
HALO = 32
VMEM_LIMIT = 56 * 1024 * 1024

F32 = jnp.float32
BF16 = jnp.bfloat16


def _cparams(*sem):
    return pltpu.CompilerParams(dimension_semantics=sem, vmem_limit_bytes=VMEM_LIMIT)


def _dot(a, b):
    return jnp.dot(a, b, preferred_element_type=F32)


def _ln(x, g, b):
    mu = jnp.mean(x, axis=-1, keepdims=True)
    xc = x - mu
    var = jnp.mean(xc * xc, axis=-1, keepdims=True)
    return xc * lax.rsqrt(var + LN_EPS) * g + b


def _sigmoid(x):
    return 1.0 / (1.0 + jnp.exp(-x))


def _silu(x):
    return x * _sigmoid(x)


def _glu_kernel(x_ref, xs_ref, wa_ref, wg_ref, ba_ref, bg_ref, u_ref, us_ref, xb_ref, xsb_ref):
    j = pl.program_id(1)
    tn = u_ref.shape[1]

    def tile(xin_ref, xbf_ref):
        @pl.when(j == 0)
        def _():
            xbf_ref[...] = xin_ref[...].astype(BF16)

        xb = xbf_ref[...]
        a = _dot(xb, wa_ref[...]) + ba_ref[...]
        g = _dot(xb, wg_ref[...]) + bg_ref[...]
        return a * _sigmoid(g)

    u_ref[...] = tile(x_ref, xb_ref)

    @pl.when(pl.program_id(0) == 0)
    def _():
        us_ref[:, pl.ds(pl.multiple_of(j * tn, tn), tn)] = tile(xs_ref, xsb_ref)


def _glu(x, xs, w1, b1, tm, tn):
    m, d = x.shape
    ms = xs.shape[0]
    nj = d // tn
    small = pl.BlockSpec((ms, d), lambda i, j: (0, 0))
    return pl.pallas_call(
        _glu_kernel,
        grid=(m // tm, nj),
        in_specs=[
            pl.BlockSpec((tm, d), lambda i, j: (i, 0)),
            small,
            pl.BlockSpec((d, tn), lambda i, j: (0, j)),
            pl.BlockSpec((d, tn), lambda i, j: (0, j + nj)),
            pl.BlockSpec((1, tn), lambda i, j: (0, j)),
            pl.BlockSpec((1, tn), lambda i, j: (0, j + nj)),
        ],
        out_specs=[pl.BlockSpec((tm, tn), lambda i, j: (i, j)), small],
        out_shape=[jax.ShapeDtypeStruct((m, d), F32), jax.ShapeDtypeStruct((ms, d), F32)],
        scratch_shapes=[pltpu.VMEM((tm, d), BF16), pltpu.VMEM((ms, d), BF16)],
        compiler_params=_cparams("arbitrary", "arbitrary"),
        name="glu",
    )(x, xs, w1, w1, b1, b1)


def _mix_tail(y, x, g_ref, b_ref, w2_ref, pg_ref, pb_ref, alpha):
    y = _silu(_ln(y, g_ref[...], b_ref[...]))
    m = _dot(y.astype(BF16), w2_ref[...])
    return _ln(alpha * x + m, pg_ref[...], pb_ref[...])


def _conv_prompt_kernel(uprev_ref, u_ref, x_ref, wdw_ref, bdw_ref, g_ref, b_ref, w2_ref,
                        pg_ref, pb_ref, o_ref, full_ref, y_ref, *, alpha, rows):
    tm, d = u_ref.shape
    first = pl.program_id(0) == 0
    full_ref[0:HALO, :] = jnp.where(first, 0.0, uprev_ref[...])
    full_ref[HALO:, :] = u_ref[...]
    off = HALO - (CONV_W - 1)

    def col_body(c, carry):
        cs = pl.ds(pl.multiple_of(c * LANES, LANES), LANES)
        w = wdw_ref[:, cs]
        bias = bdw_ref[:, cs]

        def row_body(r, carry2):
            base = pl.multiple_of(r * rows, rows)
            win = full_ref[pl.ds(base, rows + HALO), cs]
            acc = jnp.zeros((rows, LANES), F32)
            for res in range(SUBLANES):
                sh = win if res == 0 else pltpu.roll(win, rows + HALO - res, 0)
                for t in range(CONV_W):
                    if (off + t) % SUBLANES == res:
                        a8 = off + t - res
                        acc = acc + sh[a8:a8 + rows, :] * w[t:t + 1, :]
            y_ref[pl.ds(base, rows), cs] = acc + bias
            return carry2

        return lax.fori_loop(0, tm // rows, row_body, carry)

    lax.fori_loop(0, d // LANES, col_body, 0)
    o_ref[...] = _mix_tail(y_ref[...], x_ref[...], g_ref, b_ref, w2_ref, pg_ref, pb_ref, alpha)


def _conv_prompt(u, x, wdw, bdw, g, b, w2, pg, pb, alpha, tm):
    m, d = x.shape
    nh = tm // HALO
    vec = pl.BlockSpec((1, d), lambda i: (0, 0))
    return pl.pallas_call(
        functools.partial(_conv_prompt_kernel, alpha=alpha, rows=64),
        grid=(m // tm,),
        in_specs=[
            pl.BlockSpec((HALO, d), lambda i: (jnp.maximum(i * nh - 1, 0), 0)),
            pl.BlockSpec((tm, d), lambda i: (i, 0)),
            pl.BlockSpec((tm, d), lambda i: (i, 0)),
            pl.BlockSpec((CONV_W, d), lambda i: (0, 0)),
            vec, vec, vec,
            pl.BlockSpec((d, d), lambda i: (0, 0)),
            vec, vec,
        ],
        out_specs=pl.BlockSpec((tm, d), lambda i: (i, 0)),
        out_shape=jax.ShapeDtypeStruct((m, d), F32),
        scratch_shapes=[pltpu.VMEM((tm + HALO, d), F32), pltpu.VMEM((tm, d), F32)],
        compiler_params=_cparams("parallel"),
        name="conv_prompt",
    )(u, u, x, wdw, bdw, g, b, w2, pg, pb)


def _conv_sample_kernel(hist_ref, u_ref, x_ref, wdw_ref, bdw_ref, g_ref, b_ref, w2_ref,
                        pg_ref, pb_ref, o_ref, *, alpha):
    y = u_ref[...] * wdw_ref[CONV_W - 1:CONV_W, :]
    for t in range(CONV_W - 1):
        y = y + hist_ref[t] * wdw_ref[t:t + 1, :]
    y = y + bdw_ref[...]
    o_ref[...] = _mix_tail(y, x_ref[...], g_ref, b_ref, w2_ref, pg_ref, pb_ref, alpha)


def _conv_sample(hist, u, x, wdw, bdw, g, b, w2, pg, pb, alpha):
    m, d = x.shape
    return pl.pallas_call(
        functools.partial(_conv_sample_kernel, alpha=alpha),
        out_shape=jax.ShapeDtypeStruct((m, d), F32),
        compiler_params=pltpu.CompilerParams(vmem_limit_bytes=VMEM_LIMIT),
        name="conv_sample",
    )(hist, u, x, wdw, bdw, g, b, w2, pg, pb)


def _swiglu_kernel(x_ref, xs_ref, wg_ref, wu_ref, wd_ref, pg_ref, pb_ref, o_ref, os_ref, xb_ref, xsb_ref,
                   *, alpha):
    f = pl.program_id(1)

    def tile(xin_ref, out_ref, xbf_ref):
        @pl.when(f == 0)
        def _():
            xbf_ref[...] = xin_ref[...].astype(BF16)

        xb = xbf_ref[...]
        h = _silu(_dot(xb, wg_ref[...])) * _dot(xb, wu_ref[...])
        part = _dot(h.astype(BF16), wd_ref[...])

        @pl.when(f == 0)
        def _():
            out_ref[...] = part

        @pl.when(f > 0)
        def _():
            out_ref[...] += part

        @pl.when(f == pl.num_programs(1) - 1)
        def _():
            out_ref[...] = _ln(alpha * xin_ref[...] + out_ref[...], pg_ref[...], pb_ref[...])

    tile(x_ref, o_ref, xb_ref)

    @pl.when(pl.program_id(0) == 0)
    def _():
        tile(xs_ref, os_ref, xsb_ref)


def _swiglu(x, xs, wg, wu, wd, pg, pb, alpha, tm, tf):
    m, d = x.shape
    ms = xs.shape[0]
    dff = wg.shape[1]
    vec = pl.BlockSpec((1, d), lambda i, f: (0, 0))
    small = pl.BlockSpec((ms, d), lambda i, f: (0, 0))
    return pl.pallas_call(
        functools.partial(_swiglu_kernel, alpha=alpha),
        grid=(m // tm, dff // tf),
        in_specs=[
            pl.BlockSpec((tm, d), lambda i, f: (i, 0)),
            small,
            pl.BlockSpec((d, tf), lambda i, f: (0, f)),
            pl.BlockSpec((d, tf), lambda i, f: (0, f)),
            pl.BlockSpec((tf, d), lambda i, f: (f, 0)),
            vec, vec,
        ],
        out_specs=[pl.BlockSpec((tm, d), lambda i, f: (i, 0)), small],
        out_shape=[jax.ShapeDtypeStruct((m, d), F32), jax.ShapeDtypeStruct((ms, d), F32)],
        scratch_shapes=[pltpu.VMEM((tm, d), BF16), pltpu.VMEM((ms, d), BF16)],
        compiler_params=_cparams("arbitrary", "arbitrary"),
        name="swiglu",
    )(x, xs, wg, wu, wd, pg, pb)


def _ple_kernel(*refs, alpha, post_ln):
    if post_ln:
        x_ref, f_ref, pg_ref, pb_ref, p_ref, wg_ref, wp_ref, o_ref, xb_ref, xf_ref = refs
    else:
        x_ref, p_ref, wg_ref, wp_ref, o_ref, xb_ref = refs
        xf_ref = x_ref
    j = pl.program_id(1)
    tn = o_ref.shape[1]

    @pl.when(j == 0)
    def _():
        if post_ln:
            xf_ref[...] = _ln(alpha * x_ref[...] + f_ref[...], pg_ref[...], pb_ref[...])
        xb_ref[...] = xf_ref[...].astype(BF16)

    gate = _sigmoid(_dot(xb_ref[...], wg_ref[...]))
    xc = xf_ref[:, pl.ds(pl.multiple_of(j * tn, tn), tn)]
    o_ref[...] = xc + gate * _dot(p_ref[...].astype(BF16), wp_ref[...])


def _ple(x, p, wg, wp, tm, tn, post=None):
    m, d = x.shape
    pd = p.shape[1]
    row = pl.BlockSpec((tm, d), lambda i, j: (i, 0))
    vec = pl.BlockSpec((1, d), lambda i, j: (0, 0))
    tail_specs = [pl.BlockSpec((tm, pd), lambda i, j: (i, 0)),
                  pl.BlockSpec((d, tn), lambda i, j: (0, j)),
                  pl.BlockSpec((pd, tn), lambda i, j: (0, j))]
    scratch = [pltpu.VMEM((tm, d), BF16)]
    if post is None:
        args, in_specs, alpha = (x, p, wg, wp), [row] + tail_specs, 1.0
    else:
        f, pg, pb, alpha = post
        args, in_specs = (x, f, pg, pb, p, wg, wp), [row, row, vec, vec] + tail_specs
        scratch.append(pltpu.VMEM((tm, d), F32))
    return pl.pallas_call(
        functools.partial(_ple_kernel, alpha=alpha, post_ln=post is not None),
        grid=(m // tm, d // tn),
        in_specs=in_specs,
        out_specs=pl.BlockSpec((tm, tn), lambda i, j: (i, j)),
        out_shape=jax.ShapeDtypeStruct((m, d), F32),
        scratch_shapes=scratch,
        compiler_params=_cparams("parallel", "arbitrary"),
        name="ple",
    )(*args)


def _rope_tables(pos):
    half = HEAD_DIM // 2
    inv = ROPE_THETA ** (-jnp.arange(half, dtype=F32) / half)
    ang = pos.astype(F32)[:, None] * inv[None, :]
    cos, sin = jnp.cos(ang), jnp.sin(ang)
    cos_t = jnp.concatenate([cos, cos, cos, cos], axis=-1)
    sin_t = jnp.concatenate([-sin, sin, -sin, sin], axis=-1)
    return cos_t, sin_t


def _rope_tile(h, cos, sin, first_half):
    partner = jnp.where(first_half, pltpu.roll(h, LANES - HEAD_DIM // 2, 1),
                        pltpu.roll(h, HEAD_DIM // 2, 1))
    return h * cos + partner * sin


def _proj_kernel(x_ref, wk_ref, wv_ref, wq_ref, cos_ref, sin_ref, *refs, transposed):
    if transposed:
        k_ref, v_ref, kb_ref, qt_ref, vt_ref, xb_ref = refs
    else:
        k_ref, v_ref, qb_ref, xb_ref = refs

    @pl.when(pl.program_id(1) == 0)
    def _():
        xb_ref[...] = x_ref[...].astype(BF16)

    xb = xb_ref[...]
    tm, tn = k_ref.shape
    cos, sin = cos_ref[...], sin_ref[...]
    lane = lax.broadcasted_iota(jnp.int32, (tm, LANES), 1)
    first_half = (lane % HEAD_DIM) < (HEAD_DIM // 2)
    hk = _dot(xb, wk_ref[...])
    hq = _dot(xb, wq_ref[...])
    for c in range(tn // LANES):
        cs = slice(c * LANES, (c + 1) * LANES)
        kr = _rope_tile(hk[:, cs], cos, sin, first_half)
        k_ref[:, cs] = kr
        qr = _rope_tile(hq[:, cs], cos, sin, first_half)
        if transposed:
            kb_ref[:, cs] = kr.astype(BF16)
            qt_ref[cs, :] = (qr * (ATTN_SCALE * LOG2E)).T.astype(BF16)
        else:
            qb_ref[:, cs] = (qr * ATTN_SCALE).astype(BF16)
    hv = _dot(xb, wv_ref[...])
    v_ref[...] = hv
    if transposed:
        vt_ref[...] = hv.T.astype(BF16)


def _proj(x, wk, wv, wq, cos_t, sin_t, tm, tn, transposed):
    m, d = x.shape
    wspec = pl.BlockSpec((d, tn), lambda i, j: (0, j))
    tspec = pl.BlockSpec((tm, LANES), lambda i, j: (i, 0))
    ospec = pl.BlockSpec((tm, tn), lambda i, j: (i, j))
    tr_spec = pl.BlockSpec((tn, tm), lambda i, j: (j, i))
    f32_out = jax.ShapeDtypeStruct((m, d), F32)
    if transposed:
        out_specs = [ospec, ospec, ospec, tr_spec, tr_spec]
        out_shape = [f32_out, f32_out, jax.ShapeDtypeStruct((m, d), BF16),
                     jax.ShapeDtypeStruct((d, m), BF16), jax.ShapeDtypeStruct((d, m), BF16)]
    else:
        out_specs = [ospec, ospec, ospec]
        out_shape = [f32_out, f32_out, jax.ShapeDtypeStruct((m, d), BF16)]
    return pl.pallas_call(
        functools.partial(_proj_kernel, transposed=transposed),
        grid=(m // tm, d // tn),
        in_specs=[pl.BlockSpec((tm, d), lambda i, j: (i, 0)), wspec, wspec, wspec, tspec, tspec],
        out_specs=out_specs,
        out_shape=out_shape,
        scratch_shapes=[pltpu.VMEM((tm, d), BF16)],
        compiler_params=_cparams("parallel", "arbitrary"),
        name="proj",
    )(x, wk, wv, wq, cos_t, sin_t)


def _diff_lambda(lam_ref, lam_init):
    lp = lam_ref[...]
    s1 = jnp.sum(lp[0:1, :] * lp[1:2, :], axis=-1, keepdims=True)
    s2 = jnp.sum(lp[2:3, :] * lp[3:4, :], axis=-1, keepdims=True)
    return jnp.exp(s1) - jnp.exp(s2) + lam_init


def _diff_finalize(o1, o2, lam, g, lam_init):
    o = o1 - lam * o2
    o = o * lax.rsqrt(jnp.mean(o * o, axis=-1, keepdims=True) + LN_EPS)
    return o * g * (1.0 - lam_init)


def _attn_prompt_kernel(qt_ref, k_ref, vt_ref, lam_ref, g_ref, o_ref,
                        sa_ref, sb_ref, m_ref, l_ref, acc_ref, *, lam_init, tk):
    tq = qt_ref.shape[1]
    i = pl.program_id(1)
    q0 = i * tq
    n_full = q0 // tk
    qt = qt_ref[...]
    feat = lax.broadcasted_iota(jnp.int32, qt.shape, 0)
    qz = jnp.concatenate([jnp.where(feat < HEAD_DIM, qt, 0), jnp.where(feat >= HEAD_DIM, qt, 0)], axis=1)
    m_ref[...] = jnp.full(m_ref.shape, -jnp.inf, F32)
    l_ref[...] = jnp.zeros(l_ref.shape, F32)
    acc_ref[...] = jnp.zeros(acc_ref.shape, F32)

    def scores(b, dst_ref):
        base = pl.multiple_of(b * tk, tk)
        dst_ref[...] = _dot(k_ref[pl.ds(base, tk), :], qz)

    def softmax_pv(b, src_ref, masked):
        base = pl.multiple_of(b * tk, tk)
        s = src_ref[...]
        if masked:
            key = base + lax.broadcasted_iota(jnp.int32, s.shape, 0)
            qry = q0 + lax.broadcasted_iota(jnp.int32, s.shape, 1) % tq
            s = jnp.where(key <= qry, s, -jnp.inf)
        m_old = m_ref[...]
        m_new = jnp.maximum(m_old, jnp.max(s, axis=0, keepdims=True))
        a = jnp.exp2(m_old - m_new)
        p = jnp.exp2(s - m_new)
        l_ref[...] = a * l_ref[...] + jnp.sum(p, axis=0, keepdims=True)
        m_ref[...] = m_new
        acc_ref[...] = a * acc_ref[...] + _dot(vt_ref[:, pl.ds(base, tk)], p.astype(BF16))

    scores(0, sa_ref)

    def pair(t, carry):
        b = 2 * t
        scores(b + 1, sb_ref)
        softmax_pv(b, sa_ref, False)
        scores(b + 2, sa_ref)
        softmax_pv(b + 1, sb_ref, False)
        return carry

    lax.fori_loop(0, n_full // 2, pair, 0)

    @pl.when(n_full % 2 == 1)
    def _():
        scores(n_full, sb_ref)
        softmax_pv(n_full - 1, sa_ref, False)
        softmax_pv(n_full, sb_ref, True)

    @pl.when(n_full % 2 == 0)
    def _():
        softmax_pv(n_full, sa_ref, True)

    lam = _diff_lambda(lam_ref, lam_init)
    o = acc_ref[...] * (1.0 / l_ref[...])
    ot = o[:, :tq] - lam * o[:, tq:]
    ot = ot * lax.rsqrt(jnp.mean(ot * ot, axis=0, keepdims=True) + LN_EPS)
    ot = ot * g_ref[...] * (1.0 - lam_init)
    o_ref[...] = ot.T.astype(o_ref.dtype)


def _attn_prompt(qt, kb, vt, lam_p, sub_g_col, lam_init, tq, tk):
    d, s = qt.shape
    nh = d // V_DIM
    return pl.pallas_call(
        functools.partial(_attn_prompt_kernel, lam_init=lam_init, tk=tk),
        grid=(nh, s // tq),
        in_specs=[
            pl.BlockSpec((V_DIM, tq), lambda h, i: (h, i)),
            pl.BlockSpec((s, V_DIM), lambda h, i: (0, h)),
            pl.BlockSpec((V_DIM, s), lambda h, i: (h, 0)),
            pl.BlockSpec(lam_p.shape, lambda h, i: (0, 0)),
            pl.BlockSpec((V_DIM, 1), lambda h, i: (0, 0)),
        ],
        out_specs=pl.BlockSpec((tq, V_DIM), lambda h, i: (i, h)),
        out_shape=jax.ShapeDtypeStruct((s, d), BF16),
        scratch_shapes=[pltpu.VMEM((tk, 2 * tq), F32), pltpu.VMEM((tk, 2 * tq), F32),
                        pltpu.VMEM((1, 2 * tq), F32), pltpu.VMEM((1, 2 * tq), F32),
                        pltpu.VMEM((V_DIM, 2 * tq), F32)],
        compiler_params=_cparams("parallel", "arbitrary"),
        name="attn_prompt",
    )(qt, kb, vt, lam_p, sub_g_col)


def _attn_sample_kernel(pt_ref, qk_ref, vn_ref, lam_ref, g_ref, e_ref, *refs,
                        lam_init, pages_per_step):
    k_refs = refs[:pages_per_step]
    v_refs = refs[pages_per_step:2 * pages_per_step]
    o_ref = refs[2 * pages_per_step]
    qb_ref, m_ref, l_ref, acc_ref = refs[2 * pages_per_step + 1:]
    step_idx = pl.program_id(1)
    n_maps, _, page = qb_ref.shape
    n_heads = n_maps // 2

    @pl.when(step_idx == 0)
    def _():
        qk = qk_ref[0]
        qt = qk.T
        for mp in range(n_maps):
            qb_ref[mp] = jnp.broadcast_to(qt[:HEAD_DIM, mp:mp + 1], (HEAD_DIM, page))
        m_ref[...] = jnp.sum(qk[:, :HEAD_DIM] * qk[:, HEAD_DIM:], axis=-1, keepdims=True)
        l_ref[...] = jnp.ones_like(l_ref)
        vn = vn_ref[0]
        for h in range(n_heads):
            acc_ref[2 * h:2 * h + 2, :] = jnp.broadcast_to(vn[h:h + 1, :], (2, V_DIM))

    slot = lax.broadcasted_iota(jnp.int32, (n_maps, page * n_heads), 1)
    mp_row = lax.broadcasted_iota(jnp.int32, (n_maps, page * n_heads), 0)
    own_head = (slot % n_heads) == (mp_row >> 1)
    for kp_ref, vp_ref in zip(k_refs, v_refs):
        s = jnp.sum(kp_ref[0] * qb_ref[...], axis=1)
        m_old = m_ref[...]
        m_new = jnp.maximum(m_old, jnp.max(s, axis=-1, keepdims=True))
        a = jnp.exp(m_old - m_new)
        p = jnp.exp(s - m_new)
        l_ref[...] = a * l_ref[...] + jnp.sum(p, axis=-1, keepdims=True)
        m_ref[...] = m_new
        pe = _dot(p.astype(BF16), e_ref[...])
        pm = jnp.where(own_head, pe, 0.0).astype(BF16)
        vb = vp_ref[0].reshape(page * n_heads, V_DIM).astype(BF16)
        acc_ref[...] = a * acc_ref[...] + _dot(pm, vb)

    @pl.when(step_idx == pl.num_programs(1) - 1)
    def _():
        acc_ref[...] = acc_ref[...] / l_ref[...]
        o1 = acc_ref[pl.ds(0, n_heads, stride=2), :]
        o2 = acc_ref[pl.ds(1, n_heads, stride=2), :]
        lam = _diff_lambda(lam_ref, lam_init)
        o_ref[0] = _diff_finalize(o1, o2, lam, g_ref[...], lam_init).astype(o_ref.dtype)


def _attn_sample(page_table, qk3, vn3, lam_p, sub_g, cache_kt, cache_v, lam_init, pages_per_step):
    nb, n_pages = page_table.shape
    _, n_maps, hd, page = cache_kt.shape
    n_heads = cache_v.shape[2]
    pps = pages_per_step

    def kspec(j):
        return pl.BlockSpec((1, n_maps, hd, page), lambda b, s, pt: (pt[b, s * pps + j], 0, 0, 0))

    def vspec(j):
        return pl.BlockSpec((1, page, n_heads, V_DIM), lambda b, s, pt: (pt[b, s * pps + j], 0, 0, 0))

    expand = (jnp.arange(page * n_heads)[None, :] // n_heads == jnp.arange(page)[:, None]).astype(BF16)
    grid_spec = pltpu.PrefetchScalarGridSpec(
        num_scalar_prefetch=1,
        grid=(nb, n_pages // pps),
        in_specs=[
            pl.BlockSpec((1, n_maps, 2 * hd), lambda b, s, pt: (b, 0, 0)),
            pl.BlockSpec((1, n_heads, V_DIM), lambda b, s, pt: (b, 0, 0)),
            pl.BlockSpec(lam_p.shape, lambda b, s, pt: (0, 0)),
            pl.BlockSpec((1, V_DIM), lambda b, s, pt: (0, 0)),
            pl.BlockSpec(expand.shape, lambda b, s, pt: (0, 0)),
        ] + [kspec(j) for j in range(pps)] + [vspec(j) for j in range(pps)],
        out_specs=pl.BlockSpec((1, n_heads, V_DIM), lambda b, s, pt: (b, 0, 0)),
        scratch_shapes=[
            pltpu.VMEM((n_maps, hd, page), F32),
            pltpu.VMEM((n_maps, 1), F32),
            pltpu.VMEM((n_maps, 1), F32),
            pltpu.VMEM((n_maps, V_DIM), F32),
        ],
    )
    return pl.pallas_call(
        functools.partial(_attn_sample_kernel, lam_init=lam_init, pages_per_step=pps),
        grid_spec=grid_spec,
        out_shape=jax.ShapeDtypeStruct((nb, n_heads, V_DIM), BF16),
        compiler_params=_cparams("parallel", "arbitrary"),
        name="attn_sample",
    )(page_table, qk3, vn3, lam_p, sub_g, expand, *([cache_kt] * pps), *([cache_v] * pps))


def _route(xb, wr):
    n_exp = wr.shape[1]
    logits = _dot(xb, wr.astype(BF16))
    idx = lax.broadcasted_iota(jnp.int32, logits.shape, 1)
    v1 = jnp.max(logits, axis=-1, keepdims=True)
    i1 = jnp.min(jnp.where(logits == v1, idx, n_exp), axis=-1, keepdims=True)
    rest = jnp.where(idx == i1, -jnp.inf, logits)
    v2 = jnp.max(rest, axis=-1, keepdims=True)
    i2 = jnp.min(jnp.where(rest == v2, idx, n_exp), axis=-1, keepdims=True)
    t = jnp.exp(v2 - v1)
    g1 = 1.0 / (1.0 + t)
    g2 = t / (1.0 + t)
    return jnp.where(idx == i1, g1, 0.0) + jnp.where(idx == i2, g2, 0.0)


def _wo_kernel(a_ref, x_ref, as_ref, xs_ref, wo_ref, pg_ref, pb_ref, wr_ref,
               o_ref, xb_ref, comb_ref, os_ref, xsb_ref, combs_ref, *, alpha):
    def tile(ain_ref, xin_ref, out_ref, xbf_ref, cout_ref):
        m = _dot(ain_ref[...], wo_ref[...])
        x1 = _ln(alpha * xin_ref[...] + m, pg_ref[...], pb_ref[...])
        out_ref[...] = x1
        xb = x1.astype(BF16)
        xbf_ref[...] = xb
        cout_ref[...] = _route(xb, wr_ref[...])

    tile(a_ref, x_ref, o_ref, xb_ref, comb_ref)

    @pl.when(pl.program_id(0) == 0)
    def _():
        tile(as_ref, xs_ref, os_ref, xsb_ref, combs_ref)


def _wo(a, x, a_s, xs, wo, pg, pb, wr, alpha, tm):
    m, d = x.shape
    ms = xs.shape[0]
    n_exp = wr.shape[1]
    vec = pl.BlockSpec((1, d), lambda i: (0, 0))
    row = pl.BlockSpec((tm, d), lambda i: (i, 0))
    small = pl.BlockSpec((ms, d), lambda i: (0, 0))
    return pl.pallas_call(
        functools.partial(_wo_kernel, alpha=alpha),
        grid=(m // tm,),
        in_specs=[row, row, small, small, pl.BlockSpec((d, d), lambda i: (0, 0)), vec, vec,
                  pl.BlockSpec((d, n_exp), lambda i: (0, 0))],
        out_specs=[row, row, pl.BlockSpec((tm, n_exp), lambda i: (i, 0)),
                   small, small, pl.BlockSpec((ms, n_exp), lambda i: (0, 0))],
        out_shape=[jax.ShapeDtypeStruct((m, d), F32), jax.ShapeDtypeStruct((m, d), BF16),
                   jax.ShapeDtypeStruct((m, n_exp), F32),
                   jax.ShapeDtypeStruct((ms, d), F32), jax.ShapeDtypeStruct((ms, d), BF16),
                   jax.ShapeDtypeStruct((ms, n_exp), F32)],
        compiler_params=_cparams("arbitrary"),
        name="wo",
    )(a, x, a_s, xs, wo, pg, pb, wr)


def _moe_kernel(xb_ref, comb_ref, xsb_ref, combs_ref, wg_ref, wu_ref, wd_ref, o_ref, os_ref, *, sub):
    e = pl.program_id(1)

    def tile(xin_ref, cin_ref, out_ref, step):
        comb = cin_ref[...]
        idx = lax.broadcasted_iota(jnp.int32, comb.shape, 1)
        w = jnp.sum(jnp.where(idx == e, comb, 0.0), axis=-1, keepdims=True)

        @pl.when(e == 0)
        def _():
            out_ref[...] = jnp.zeros(out_ref.shape, F32)

        for r0 in range(0, xin_ref.shape[0], step):
            rows = slice(r0, r0 + step)
            xb = xin_ref[rows, :]
            h = _silu(_dot(xb, wg_ref[0])) * _dot(xb, wu_ref[0])
            out_ref[rows, :] += _dot((h * w[rows, :]).astype(BF16), wd_ref[0])

    tile(xb_ref, comb_ref, o_ref, sub)

    @pl.when(pl.program_id(0) == 0)
    def _():
        tile(xsb_ref, combs_ref, os_ref, xsb_ref.shape[0])


def _moe(xb, comb, xsb, combs, wg, wu, wd, tm):
    m, d = xb.shape
    ms = xsb.shape[0]
    n_exp, _, dexp = wg.shape
    return pl.pallas_call(
        functools.partial(_moe_kernel, sub=min(tm, MOE_SUB_ROWS)),
        grid=(m // tm, n_exp),
        in_specs=[
            pl.BlockSpec((tm, d), lambda i, e: (i, 0)),
            pl.BlockSpec((tm, n_exp), lambda i, e: (i, 0)),
            pl.BlockSpec((ms, d), lambda i, e: (0, 0)),
            pl.BlockSpec((ms, n_exp), lambda i, e: (0, 0)),
            pl.BlockSpec((1, d, dexp), lambda i, e: (e, 0, 0)),
            pl.BlockSpec((1, d, dexp), lambda i, e: (e, 0, 0)),
            pl.BlockSpec((1, dexp, d), lambda i, e: (e, 0, 0)),
        ],
        out_specs=[pl.BlockSpec((tm, d), lambda i, e: (i, 0)), pl.BlockSpec((ms, d), lambda i, e: (0, 0))],
        out_shape=[jax.ShapeDtypeStruct((m, d), F32), jax.ShapeDtypeStruct((ms, d), F32)],
        compiler_params=_cparams("arbitrary", "arbitrary"),
        name="moe",
    )(xb, comb, xsb, combs, wg, wu, wd)


def kernel(x_prompt, x_sample, state_conv, cache_k, cache_v, page_table, p_prompt, p_sample,
           conv_w_pw1, conv_b_pw1, conv_w_dw, conv_b_dw, conv_ln_g, conv_ln_b, conv_w_pw2,
           kv_w_k, kv_w_v, attn_w_q, attn_lambda, attn_sub_g, attn_w_o,
           ffn_w_gate, ffn_w_up, ffn_w_down, moe_w_router, moe_w_gate, moe_w_up, moe_w_down,
           post_ln_g, post_ln_b, ple_w_proj, ple_w_gate):
    bp, sp, d = x_prompt.shape
    bd, sd, _ = x_sample.shape
    depth = post_ln_g.shape[0]
    n_a = conv_w_pw1.shape[0]
    assert bp == 1 and sd == 1 and depth == 2 and n_a == 1
    n_heads = d // V_DIM
    n_maps = 2 * n_heads
    past = page_table.shape[1] * cache_k.shape[1]
    alpha = (2.0 * depth) ** 0.25
    lam_init = 0.8 - 0.6 * math.exp(-0.3 * 1)

    bf = lambda w: w.astype(BF16)
    row = lambda v: v.reshape(1, -1)

    xp = x_prompt.reshape(sp, d)
    xd = x_sample.reshape(bd, d)
    pp = p_prompt.reshape(depth, sp, -1)
    pdm = p_sample.reshape(depth, bd, -1)

    w1, w2 = bf(conv_w_pw1[0]), bf(conv_w_pw2[0])
    b1 = row(conv_b_pw1[0])
    conv_args = (conv_w_dw[0], row(conv_b_dw[0]), row(conv_ln_g[0]), row(conv_ln_b[0]), w2,
                 row(post_ln_g[0, 0]), row(post_ln_b[0, 0]), alpha)
    up, ud = _glu(xp, xd, w1, b1, 1024, 512)
    conv_prompt = up[sp - (CONV_W - 1):].reshape(1, 1, CONV_W - 1, d)
    hist = jnp.transpose(state_conv[0], (1, 0, 2))
    conv_sample = jnp.transpose(jnp.concatenate([hist[1:], ud[None]], axis=0), (1, 0, 2))[None]
    xp = _conv_prompt(up, xp, *conv_args, 512)
    xd = _conv_sample(hist, ud, xd, *conv_args)

    fw = (bf(ffn_w_gate[0]), bf(ffn_w_up[0]), bf(ffn_w_down[0]), row(post_ln_g[0, 1]), row(post_ln_b[0, 1]), alpha)
    xp, xd = _swiglu(xp, xd, *fw, 512, 512)

    pw = (bf(ple_w_gate[0]), bf(ple_w_proj[0]))
    xp = _ple(xp, pp[0], *pw, 1024, 512)
    xd = _ple(xd, pdm[0], *pw, bd, 512)

    wk, wv, wq = bf(kv_w_k), bf(kv_w_v), bf(attn_w_q[0])
    cos_p, sin_p = _rope_tables(jnp.arange(sp))
    cos_d, sin_d = _rope_tables(jnp.full((bd,), past))
    k_p, v_p, kb_p, qt_p, vt_p = _proj(xp, wk, wv, wq, cos_p, sin_p, 1024, 512, True)
    k_d, v_d, qb_d = _proj(xd, wk, wv, wq, cos_d, sin_d, bd, 512, False)

    lam_p = attn_lambda[0]
    sub_g = row(attn_sub_g[0])
    ap = _attn_prompt(qt_p, kb_p, vt_p, lam_p, attn_sub_g[0].reshape(V_DIM, 1), lam_init, 512, 512)
    cache_kt = jnp.transpose(cache_k, (0, 2, 3, 1))
    qk3 = jnp.concatenate([qb_d.astype(F32).reshape(bd, n_maps, HEAD_DIM),
                           k_d.astype(BF16).astype(F32).reshape(bd, n_maps, HEAD_DIM)], axis=-1)
    ad = _attn_sample(page_table, qk3, v_d.reshape(bd, n_heads, V_DIM), lam_p, sub_g,
                      cache_kt, cache_v, lam_init, 8).reshape(bd, d)
    wo_args = (bf(attn_w_o[0]), row(post_ln_g[1, 0]), row(post_ln_b[1, 0]), moe_w_router[0], alpha)
    xp, xbp, combp, xd, xbd, combd = _wo(ap, xp, ad, xd, *wo_args, 512)

    mw = (bf(moe_w_gate[0]), bf(moe_w_up[0]), bf(moe_w_down[0]))
    fp, fd = _moe(xbp, combp, xbd, combd, *mw, 512)

    pw = (bf(ple_w_gate[1]), bf(ple_w_proj[1]))
    post = (row(post_ln_g[1, 1]), row(post_ln_b[1, 1]), alpha)
    xp = _ple(xp, pp[1], *pw, 512, 512, post=(fp,) + post)
    xd = _ple(xd, pdm[1], *pw, bd, 512, post=(fd,) + post)

    return (xp.reshape(bp, sp, d), xd.reshape(bd, sd, d), conv_prompt, conv_sample,
            k_p.reshape(bp, sp, n_maps, HEAD_DIM), v_p.reshape(bp, sp, n_heads, V_DIM),
            k_d.reshape(bd, sd, n_maps, HEAD_DIM), v_d.reshape(bd, sd, n_heads, V_DIM))
```
